```python
import math
import jax, jax.numpy as jnp
from jax import lax
import numpy as np

D_MODEL = 1024
BATCH = 4
SEQ = 4096
DEPTH = 4

HEAD_DIM = 64
ROPE_THETA = 10000.0
Q_BLOCK = 128
NORM_EPS = 1e-6
NEG_INF = -1e30
BIG = 1e30

NSA_HEADS = 8
NSA_GROUPS = 2
NSA_HPG = NSA_HEADS // NSA_GROUPS
CMP_BLOCK = 32
CMP_STRIDE = 16
SEL_BLOCK = 64
SEL_TOPK = 16
WINDOW = 512
NSA_Q_CHUNK = 64

DIFF_HEADS = 4
DIFF_VDIM = 2 * HEAD_DIM

SB_HEADS = D_MODEL // HEAD_DIM
SB_WIDTH = SB_HEADS * HEAD_DIM

FFN_HIDDEN = -(-(8 * D_MODEL) // (3 * 256)) * 256

NSA_Q_W = NSA_HEADS * HEAD_DIM
NSA_KV_W = NSA_GROUPS * HEAD_DIM
NSA_GATE_W = 3 * NSA_HEADS
DIFF_QK_W = DIFF_HEADS * 2 * HEAD_DIM
DIFF_V_W = DIFF_HEADS * DIFF_VDIM
EVEN_SIZES = (NSA_Q_W, NSA_KV_W, NSA_KV_W, NSA_KV_W, NSA_KV_W, NSA_KV_W, NSA_KV_W,
              NSA_GATE_W, DIFF_QK_W, DIFF_QK_W, DIFF_V_W)
EVEN_IN = sum(EVEN_SIZES)
EVEN_OUT = NSA_HEADS * HEAD_DIM + DIFF_HEADS * DIFF_VDIM

N_EVEN = (DEPTH + 1) // 2
N_ODD = DEPTH // 2

kernel_name = "nsa_diff_stickbreaking_hybrid"


def rmsnorm(x, g):
    xf = x.astype(jnp.float32)
    y = xf * lax.rsqrt(jnp.mean(xf * xf, axis=-1, keepdims=True) + NORM_EPS)
    return (y * g.astype(jnp.float32)).astype(x.dtype)


def rope_tables(seq, dim):
    inv = 1.0 / (ROPE_THETA ** (jnp.arange(0, dim, 2, dtype=jnp.float32) / dim))
    ang = jnp.arange(seq, dtype=jnp.float32)[:, None] * inv[None, :]
    return jnp.cos(ang), jnp.sin(ang)


def apply_rope(x, cos, sin):
    x1, x2 = jnp.split(x, 2, axis=-1)
    c = cos[None, :, None, :].astype(x.dtype)
    s = sin[None, :, None, :].astype(x.dtype)
    return jnp.concatenate([x1 * c - x2 * s, x1 * s + x2 * c], axis=-1)


def masked_softmax(s, mask):
    s = jnp.where(mask, s.astype(jnp.float32), NEG_INF)
    return jnp.where(mask, jax.nn.softmax(s, axis=-1), 0.0)


def nsa_mixer(q, kc, vc, ks, vs, kw, vw, gate_logits,
              cmp_pos_k, cmp_w_k, cmp_pos_v, cmp_w_v, cos, sin):
    B, S = q.shape[0], q.shape[1]
    G, HPG, D = NSA_GROUPS, NSA_HPG, HEAD_DIM
    scale = D ** -0.5
    q = apply_rope(q, cos, sin)
    kc, ks, kw = apply_rope(kc, cos, sin), apply_rope(ks, cos, sin), apply_rope(kw, cos, sin)
    qg = q.reshape(B, S, G, HPG, D).transpose(0, 2, 3, 1, 4)
    kc, vc, ks, vs, kw, vw = [a.transpose(0, 2, 1, 3) for a in (kc, vc, ks, vs, kw, vw)]
    pos = jnp.arange(S)

    n_cmp = (S - CMP_BLOCK) // CMP_STRIDE + 1
    cmp_start = jnp.arange(n_cmp) * CMP_STRIDE
    cmp_idx = cmp_start[:, None] + jnp.arange(CMP_BLOCK)[None, :]

    def compress(a, p, w):
        blocks = a[:, :, cmp_idx] + p
        return blocks.reshape(B, G, n_cmp, CMP_BLOCK * D) @ w

    k_cmp = compress(kc, cmp_pos_k, cmp_w_k)
    v_cmp = compress(vc, cmp_pos_v, cmp_w_v)
    s_cmp = jnp.einsum('bghsd,bgnd->bghsn', qg, k_cmp) * scale
    cmp_mask = (cmp_start + CMP_BLOCK - 1)[None, :] <= pos[:, None]
    p_cmp = masked_softmax(s_cmp, cmp_mask)
    o_cmp = jnp.einsum('bghsn,bgnd->bghsd', p_cmp.astype(v_cmp.dtype), v_cmp)

    n_slc = S // SEL_BLOCK
    sel_start = jnp.arange(n_slc) * SEL_BLOCK
    cmp_to_sel = ((cmp_start[:, None] < sel_start[None, :] + SEL_BLOCK) &
                  (cmp_start[:, None] + CMP_BLOCK > sel_start[None, :])).astype(jnp.float32)
    imp = jnp.einsum('bghsn,nj->bgsj', p_cmp, cmp_to_sel)
    qblk = pos // SEL_BLOCK
    jb = jnp.arange(n_slc)
    forced = (jb[None, :] == 0) | (jb[None, :] == qblk[:, None]) | (jb[None, :] == qblk[:, None] - 1)
    future = jb[None, :] > qblk[:, None]
    imp = jnp.where(forced, BIG, jnp.where(future, -BIG, imp))
    n_top = min(SEL_TOPK, n_slc)
    top_val, top_idx = lax.top_k(imp, n_top)
    top_ok = top_val >= 0.0

    Qc = NSA_Q_CHUNK
    nc = S // Qc
    ks_blk = ks.reshape(B, G, n_slc, SEL_BLOCK, D)
    vs_blk = vs.reshape(B, G, n_slc, SEL_BLOCK, D)
    kw_pad = jnp.pad(kw, ((0, 0), (0, 0), (WINDOW, 0), (0, 0)))
    vw_pad = jnp.pad(vw, ((0, 0), (0, 0), (WINDOW, 0), (0, 0)))
    q_ch = qg.reshape(B, G, HPG, nc, Qc, D).transpose(3, 0, 1, 2, 4, 5)
    idx_ch = top_idx.reshape(B, G, nc, Qc, n_top).transpose(2, 0, 1, 3, 4)
    ok_ch = top_ok.reshape(B, G, nc, Qc, n_top).transpose(2, 0, 1, 3, 4)
    bi = jnp.arange(B)[:, None, None, None]
    gi = jnp.arange(G)[None, :, None, None]
    offs = jnp.arange(SEL_BLOCK)
    n_sel_keys = n_top * SEL_BLOCK

    def chunk(args):
        c, q_c, idx_c, ok_c = args
        t = c * Qc + jnp.arange(Qc)
        k_sel = ks_blk[bi, gi, idx_c].reshape(B, G, Qc, n_sel_keys, D)
        v_sel = vs_blk[bi, gi, idx_c].reshape(B, G, Qc, n_sel_keys, D)
        kpos = idx_c[..., None] * SEL_BLOCK + offs
        m_sel = (ok_c[..., None] & (kpos <= t[None, None, :, None, None])).reshape(B, G, Qc, n_sel_keys)
        s_sel = jnp.einsum('bghqd,bgqnd->bghqn', q_c, k_sel) * scale
        p_sel = masked_softmax(s_sel, m_sel[:, :, None])
        o_sel = jnp.einsum('bghqn,bgqnd->bghqd', p_sel.astype(v_sel.dtype), v_sel)
        k_w = lax.dynamic_slice_in_dim(kw_pad, c * Qc, Qc + WINDOW, axis=2)
        v_w = lax.dynamic_slice_in_dim(vw_pad, c * Qc, Qc + WINDOW, axis=2)
        wpos = c * Qc - WINDOW + jnp.arange(Qc + WINDOW)
        m_w = ((wpos[None, :] <= t[:, None]) & (wpos[None, :] > t[:, None] - WINDOW)
               & (wpos[None, :] >= 0))
        s_w = jnp.einsum('bghqd,bgkd->bghqk', q_c, k_w) * scale
        p_w = masked_softmax(s_w, m_w)
        o_w = jnp.einsum('bghqk,bgkd->bghqd', p_w.astype(v_w.dtype), v_w)
        return o_sel, o_w

    o_sel, o_win = lax.map(chunk, (jnp.arange(nc), q_ch, idx_ch, ok_ch))
    o_sel = o_sel.transpose(1, 0, 4, 2, 3, 5).reshape(B, S, NSA_HEADS, D)
    o_win = o_win.transpose(1, 0, 4, 2, 3, 5).reshape(B, S, NSA_HEADS, D)
    o_cmp = o_cmp.transpose(0, 3, 1, 2, 4).reshape(B, S, NSA_HEADS, D)

    g = jax.nn.sigmoid(gate_logits)
    o = g[..., 0:1] * o_cmp + g[..., 1:2] * o_sel + g[..., 2:3] * o_win
    return o.reshape(B, S, NSA_HEADS * D)


def diff_mixer(q, k, v, lq1, lk1, lq2, lk2, subln, lam_init, cos, sin):
    B, S = q.shape[0], q.shape[1]
    H, D = DIFF_HEADS, HEAD_DIM
    scale = D ** -0.5
    heads = lambda a: a.transpose(0, 2, 1, 3)
    q1 = heads(apply_rope(q[:, :, :, 0], cos, sin))
    q2 = heads(apply_rope(q[:, :, :, 1], cos, sin))
    k1 = heads(apply_rope(k[:, :, :, 0], cos, sin))
    k2 = heads(apply_rope(k[:, :, :, 1], cos, sin))
    vh = heads(v)
    lam = (jnp.exp(jnp.sum(lq1.astype(jnp.float32) * lk1.astype(jnp.float32)))
           - jnp.exp(jnp.sum(lq2.astype(jnp.float32) * lk2.astype(jnp.float32))) + lam_init)
    nb = S // Q_BLOCK
    q1b = q1.reshape(B, H, nb, Q_BLOCK, D).transpose(2, 0, 1, 3, 4)
    q2b = q2.reshape(B, H, nb, Q_BLOCK, D).transpose(2, 0, 1, 3, 4)
    kpos = jnp.arange(S)

    def blk(args):
        i, a1, a2 = args
        t = i * Q_BLOCK + jnp.arange(Q_BLOCK)
        m = kpos[None, :] <= t[:, None]
        p1 = masked_softmax(jnp.einsum('bhqd,bhkd->bhqk', a1, k1) * scale, m)
        p2 = masked_softmax(jnp.einsum('bhqd,bhkd->bhqk', a2, k2) * scale, m)
        attn = p1 - lam * p2
        return jnp.einsum('bhqk,bhkd->bhqd', attn.astype(vh.dtype), vh)

    o = lax.map(blk, (jnp.arange(nb), q1b, q2b))
    o = o.transpose(1, 0, 3, 2, 4).reshape(B, S, H, DIFF_VDIM)
    o = rmsnorm(o, subln) * (1.0 - lam_init)
    return o.reshape(B, S, H * DIFF_VDIM)


def even_mixer(h, w_in, w_out, cmp_pos_k, cmp_w_k, cmp_pos_v, cmp_w_v,
               lq1, lk1, lq2, lk2, subln, lam_init, cos, sin):
    B, S, _ = h.shape
    splits = np.cumsum(EVEN_SIZES)[:-1].tolist()
    parts = jnp.split(h @ w_in, splits, axis=-1)
    nq, kc, vc, ks, vs, kw, vw, gl, dq, dk, dv = parts
    kvs = [a.reshape(B, S, NSA_GROUPS, HEAD_DIM) for a in (kc, vc, ks, vs, kw, vw)]
    o_nsa = nsa_mixer(nq.reshape(B, S, NSA_HEADS, HEAD_DIM), *kvs,
                      gl.reshape(B, S, NSA_HEADS, 3),
                      cmp_pos_k, cmp_w_k, cmp_pos_v, cmp_w_v, cos, sin)
    o_diff = diff_mixer(dq.reshape(B, S, DIFF_HEADS, 2, HEAD_DIM),
                        dk.reshape(B, S, DIFF_HEADS, 2, HEAD_DIM),
                        dv.reshape(B, S, DIFF_HEADS, DIFF_VDIM),
                        lq1, lk1, lq2, lk2, subln, lam_init, cos, sin)
    return jnp.concatenate([o_nsa, o_diff], axis=-1) @ w_out


def sb_mixer(h, w_in, w_out):
    B, S, _ = h.shape
    scale = HEAD_DIM ** -0.5
    q, k, v = jnp.split(h @ w_in, 3, axis=-1)
    heads = lambda a: a.reshape(B, S, SB_HEADS, HEAD_DIM).transpose(0, 2, 1, 3)
    q, k, v = heads(q), heads(k), heads(v)
    nb = S // Q_BLOCK
    qb = q.reshape(B, SB_HEADS, nb, Q_BLOCK, HEAD_DIM).transpose(2, 0, 1, 3, 4)
    kpos = jnp.arange(S)

    def blk(args):
        i, q_c = args
        t = i * Q_BLOCK + jnp.arange(Q_BLOCK)
        strict = kpos[None, :] < t[:, None]
        z = jnp.einsum('bhqd,bhkd->bhqk', q_c, k).astype(jnp.float32) * scale
        log_rest = jnp.where(strict, jax.nn.log_sigmoid(-z), 0.0)
        suffix = lax.cumsum(log_rest, axis=3, reverse=True) - log_rest
        a = jnp.where(strict, jnp.exp(jax.nn.log_sigmoid(z) + suffix), 0.0)
        return jnp.einsum('bhqk,bhkd->bhqd', a.astype(v.dtype), v)

    o = lax.map(blk, (jnp.arange(nb), qb))
    o = o.transpose(1, 0, 3, 2, 4).reshape(B, S, SB_WIDTH)
    return o @ w_out


def swiglu(h, w_gate, w_up, w_down):
    return (jax.nn.silu(h @ w_gate) * (h @ w_up)) @ w_down


def _nrm(key, shape, scale):
    return jax.random.normal(key, shape, jnp.float32) * scale


def setup_inputs(seed: int = 0) -> dict:
    key = jax.random.key(seed)
    ks = jax.random.split(key, 24)
    d = D_MODEL
    return {
        "x": _nrm(ks[0], (BATCH, SEQ, d), 1.0),
        "norm_mix": 1.0 + _nrm(ks[1], (DEPTH, d), 0.02),
        "norm_ffn": 1.0 + _nrm(ks[2], (DEPTH, d), 0.02),
        "norm_final": 1.0 + _nrm(ks[3], (d,), 0.02),
        "even_w_in": _nrm(ks[4], (N_EVEN, d, EVEN_IN), d ** -0.5),
        "even_w_out": _nrm(ks[5], (N_EVEN, EVEN_OUT, d), EVEN_OUT ** -0.5),
        "cmp_pos_k": _nrm(ks[6], (N_EVEN, CMP_BLOCK, HEAD_DIM), 0.3),
        "cmp_w_k": _nrm(ks[7], (N_EVEN, CMP_BLOCK * HEAD_DIM, HEAD_DIM), (CMP_BLOCK * HEAD_DIM) ** -0.5),
        "cmp_pos_v": _nrm(ks[8], (N_EVEN, CMP_BLOCK, HEAD_DIM), 0.3),
        "cmp_w_v": _nrm(ks[9], (N_EVEN, CMP_BLOCK * HEAD_DIM, HEAD_DIM), (CMP_BLOCK * HEAD_DIM) ** -0.5),
        "diff_lq1": _nrm(ks[10], (N_EVEN, HEAD_DIM), 0.1),
        "diff_lk1": _nrm(ks[11], (N_EVEN, HEAD_DIM), 0.1),
        "diff_lq2": _nrm(ks[12], (N_EVEN, HEAD_DIM), 0.1),
        "diff_lk2": _nrm(ks[13], (N_EVEN, HEAD_DIM), 0.1),
        "diff_subln": 1.0 + _nrm(ks[14], (N_EVEN, DIFF_VDIM), 0.02),
        "odd_w_in": _nrm(ks[15], (N_ODD, d, 3 * SB_WIDTH), d ** -0.5),
        "odd_w_out": _nrm(ks[16], (N_ODD, SB_WIDTH, d), SB_WIDTH ** -0.5),
        "ffn_w_gate": _nrm(ks[17], (DEPTH, d, FFN_HIDDEN), d ** -0.5),
        "ffn_w_up": _nrm(ks[18], (DEPTH, d, FFN_HIDDEN), d ** -0.5),
        "ffn_w_down": _nrm(ks[19], (DEPTH, FFN_HIDDEN, d), FFN_HIDDEN ** -0.5),
    }


def reference(x, norm_mix, norm_ffn, norm_final, even_w_in, even_w_out,
              cmp_pos_k, cmp_w_k, cmp_pos_v, cmp_w_v,
              diff_lq1, diff_lk1, diff_lq2, diff_lk2, diff_subln,
              odd_w_in, odd_w_out, ffn_w_gate, ffn_w_up, ffn_w_down):
    S = x.shape[1]
    cos, sin = rope_tables(S, HEAD_DIM)
    for layer in range(DEPTH):
        h = rmsnorm(x, norm_mix[layer])
        if layer % 2 == 0:
            e = layer // 2
            lam_init = 0.8 - 0.6 * math.exp(-0.3 * layer)
            y = even_mixer(h, even_w_in[e], even_w_out[e],
                           cmp_pos_k[e], cmp_w_k[e], cmp_pos_v[e], cmp_w_v[e],
                           diff_lq1[e], diff_lk1[e], diff_lq2[e], diff_lk2[e], diff_subln[e],
                           lam_init, cos, sin)
        else:
            o = layer // 2
            y = sb_mixer(h, odd_w_in[o], odd_w_out[o])
        x = x + y
        h = rmsnorm(x, norm_ffn[layer])
        x = x + swiglu(h, ffn_w_gate[layer], ffn_w_up[layer], ffn_w_down[layer])
    return rmsnorm(x, norm_final)
```

```python
import functools
import math

import numpy as np
import jax
import jax.numpy as jnp
from jax import lax
from jax.experimental import pallas as pl
from jax.experimental.pallas import tpu as pltpu

D_MODEL = 1024
DEPTH = 4
HEAD_DIM = 64
ROPE_THETA = 10000.0
NORM_EPS = 1e-6
NEG_INF = -1e30
BIG = 1e30

NSA_HEADS = 8
NSA_GROUPS = 2
NSA_HPG = NSA_HEADS // NSA_GROUPS
CMP_BLOCK = 32
CMP_STRIDE = 16
SEL_BLOCK = 64
SEL_TOPK = 16
WINDOW = 512

DIFF_HEADS = 4
DIFF_VDIM = 2 * HEAD_DIM

SB_HEADS = D_MODEL // HEAD_DIM
FFN_HIDDEN = -(-(8 * D_MODEL) // (3 * 256)) * 256

NSA_Q_W = NSA_HEADS * HEAD_DIM
NSA_KV_W = NSA_GROUPS * HEAD_DIM
NSA_GATE_W = 3 * NSA_HEADS
DIFF_QK_W = DIFF_HEADS * 2 * HEAD_DIM
DIFF_V_W = DIFF_HEADS * DIFF_VDIM

LANES = 128
VMEM_LIMIT = 56 * 1024 * 1024

BF16 = jnp.bfloat16
F32 = jnp.float32


def _dot(a, b):
    return jnp.dot(a, b, preferred_element_type=F32)


def _dot_nt(a, b):
    return lax.dot_general(a, b, (((1,), (1,)), ((), ())), preferred_element_type=F32)


def _rms(x, g):
    return x * lax.rsqrt(jnp.mean(x * x, axis=-1, keepdims=True) + NORM_EPS) * g


def _params(sem, vmem=VMEM_LIMIT):
    return pltpu.CompilerParams(dimension_semantics=sem, vmem_limit_bytes=vmem)


def _resident(shape):
    nd = len(shape)
    return pl.BlockSpec(shape, lambda *_: (0,) * nd, pipeline_mode=pl.Buffered(1))


def _inproj_kernel(x_ref, g_ref, w_ref, cos_ref, sin_ref, *out_refs, widths, n_rope, gate_last):
    h = _rms(x_ref[...], g_ref[...]).astype(BF16)
    lane = lax.broadcasted_iota(jnp.int32, (1, LANES), 1)
    first_half = (lane % HEAD_DIM) < (HEAD_DIM // 2)
    col = 0
    for oi, (o_ref, w) in enumerate(zip(out_refs, widths)):
        y = _dot(h, w_ref[:, col:col + w])
        if oi < n_rope:
            cos, sin = cos_ref[...], sin_ref[...]
            for b in range(w // LANES):
                yb = y[:, b * LANES:(b + 1) * LANES]
                partner = jnp.where(first_half, pltpu.roll(yb, LANES - HEAD_DIM // 2, 1),
                                    pltpu.roll(yb, HEAD_DIM // 2, 1))
                o_ref[:, b * LANES:(b + 1) * LANES] = (yb * cos + partner * sin).astype(o_ref.dtype)
        else:
            o_ref[...] = y.astype(o_ref.dtype)
        col += w


def _inproj(x2, g, w, cos_t, sin_t, widths, n_rope, gate_last, seq, tm=512):
    T = x2.shape[0]
    n_pos_blocks = seq // tm
    out_shape = [jax.ShapeDtypeStruct((T, wd), F32 if (gate_last and i == len(widths) - 1) else BF16)
                 for i, wd in enumerate(widths)]
    out_specs = [pl.BlockSpec((tm, wd), lambda m: (m, 0)) for wd in widths]
    return pl.pallas_call(
        functools.partial(_inproj_kernel, widths=tuple(widths), n_rope=n_rope, gate_last=gate_last),
        grid=(T // tm,),
        in_specs=[pl.BlockSpec((tm, D_MODEL), lambda m: (m, 0)),
                  _resident((1, D_MODEL)),
                  _resident(w.shape),
                  pl.BlockSpec((tm, LANES), lambda m: (m % n_pos_blocks, 0)),
                  pl.BlockSpec((tm, LANES), lambda m: (m % n_pos_blocks, 0))],
        out_specs=out_specs,
        out_shape=out_shape,
        compiler_params=_params(("parallel",)),
        name="inproj",
    )(x2, g.reshape(1, D_MODEL), w, cos_t, sin_t)


def _compress_kernel(kc_ref, vc_ref, pk_ref, pv_ref, wk_ref, wv_ref, ko_ref, vo_ref):
    def one(x_ref, p_ref, w_ref, o_ref):
        x = x_ref[...].astype(F32)
        first = _dot((x + p_ref[0:1, :]).astype(BF16), w_ref[0])
        second = _dot((x + p_ref[1:2, :]).astype(BF16), w_ref[1])
        n = second.shape[0]
        o_ref[...] = (first + pltpu.roll(second, n - 1, 0)).astype(o_ref.dtype)

    one(kc_ref, pk_ref, wk_ref, ko_ref)
    one(vc_ref, pv_ref, wv_ref, vo_ref)


def _compress(kc, vc, pk, pv, wk, wv, batch, seq):
    rows = seq // CMP_STRIDE
    width = CMP_STRIDE * LANES
    kc3 = kc.reshape(batch, rows, width)
    vc3 = vc.reshape(batch, rows, width)
    blk = pl.BlockSpec((None, rows, width), lambda b: (b, 0, 0))
    oblk = pl.BlockSpec((None, rows, LANES), lambda b: (b, 0, 0))
    return pl.pallas_call(
        _compress_kernel,
        grid=(batch,),
        in_specs=[blk, blk, _resident(pk.shape), _resident(pv.shape), _resident(wk.shape), _resident(wv.shape)],
        out_specs=[oblk, oblk],
        out_shape=[jax.ShapeDtypeStruct((batch, rows, LANES), BF16)] * 2,
        compiler_params=_params(("parallel",)),
        name="nsa_compress",
    )(kc3, vc3, pk, pv, wk, wv)


def _online_step(s, v, m, l, acc):
    m_new = jnp.maximum(m, jnp.max(s, axis=-1, keepdims=True))
    p = jnp.exp(s - m_new)
    alpha = jnp.exp(m - m_new)
    l = alpha * l + jnp.sum(p, axis=-1, keepdims=True)
    acc = alpha * acc + _dot(p.astype(BF16), v)
    return m_new, l, acc


def _nsa_kernel(q_ref, kcmp_ref, vcmp_ref, ks_ref, vs_ref, kw_ref, vw_ref, gl_ref, ct_ref, e_ref,
                o_ref, bias_ref, *, tq, n_cmp_rows, n_slc):
    qi = pl.program_id(1)
    q0 = qi * tq
    tk = tq
    n_kt_max = bias_ref.shape[1]
    lane = lax.broadcasted_iota(jnp.int32, (1, LANES), 1)
    half = [lane < HEAD_DIM, lane >= HEAD_DIM]
    t_col = q0 + lax.broadcasted_iota(jnp.int32, (tq, 1), 0)

    qm = [[jnp.where(half[g], q_ref[:, i * LANES:(i + 1) * LANES], jnp.zeros((), BF16))
           for i in range(NSA_HPG)] for g in range(NSA_GROUPS)]

    n_idx = lax.broadcasted_iota(jnp.int32, (1, n_cmp_rows), 1)
    cmp_ok = (n_idx * CMP_STRIDE + (CMP_BLOCK - 1)) <= t_col
    kcmp = kcmp_ref[...]
    vcmp = vcmp_ref[...]
    o_cmp = [[None] * NSA_HPG for _ in range(NSA_GROUPS)]
    p_sum = []
    for g in range(NSA_GROUPS):
        acc_p = jnp.zeros((tq, n_cmp_rows), F32)
        for i in range(NSA_HPG):
            s = jnp.where(cmp_ok, _dot_nt(qm[g][i], kcmp), NEG_INF)
            m = jnp.max(s, axis=-1, keepdims=True)
            e = jnp.where(cmp_ok, jnp.exp(s - m), 0.0)
            l = jnp.sum(e, axis=-1, keepdims=True)
            p = e * jnp.where(l > 0.0, 1.0 / l, 0.0)
            acc_p = acc_p + p
            o_cmp[g][i] = _dot(p.astype(BF16), vcmp)
        p_sum.append(acc_p)

    jb = lax.broadcasted_iota(jnp.int32, (n_slc, 1), 0)
    qblk = (q0 + lax.broadcasted_iota(jnp.int32, (1, tq), 1)) // SEL_BLOCK
    forced = (jb == 0) | (jb == qblk) | (jb == qblk - 1)
    future = jb > qblk
    key_in_tile = lax.broadcasted_iota(jnp.int32, (1, tk), 1)
    ct = ct_ref[...]
    for g in range(NSA_GROUPS):
        r0 = p_sum[g]
        p_hi = r0.astype(BF16)
        r1 = r0 - p_hi.astype(F32)
        p_mid = r1.astype(BF16)
        p_lo = (r1 - p_mid.astype(F32)).astype(BF16)
        imp = _dot_nt(ct, p_hi) + _dot_nt(ct, p_mid) + _dot_nt(ct, p_lo)
        imp = jnp.where(forced, BIG, jnp.where(future, -BIG, imp))
        cnt = jnp.zeros((n_slc, tq), F32)
        for j in range(n_slc):
            row = imp[j:j + 1, :]
            beats = (row > imp) | ((row == imp) & (jb > j))
            cnt = cnt + jnp.where(beats, 1.0, 0.0)
        sel_t = jnp.where((cnt < float(SEL_TOPK)) & (imp >= 0.0), 1.0, 0.0)
        sel_t = jnp.concatenate([sel_t, jnp.zeros((LANES - n_slc, tq), F32)], axis=0)
        sel = sel_t.T.astype(BF16)
        for kt in range(n_kt_max):
            @pl.when(kt <= qi)
            def _():
                hit = _dot(sel, e_ref[:, kt * tk:(kt + 1) * tk])
                ok = (hit > 0.5) & ((kt * tk + key_in_tile) <= t_col)
                bias_ref[g, kt] = jnp.where(ok, 0.0, NEG_INF)

    def sel_head(g, i):
        q = qm[g][i]

        def body(kt, carry):
            start = pl.multiple_of(kt * tk, tk)
            s = _dot_nt(q, ks_ref[pl.ds(start, tk), :]) + bias_ref[g, kt]
            return _online_step(s, vs_ref[pl.ds(start, tk), :], *carry)

        init = (jnp.full((tq, 1), NEG_INF, F32), jnp.zeros((tq, 1), F32), jnp.zeros((tq, LANES), F32))
        _, l, acc = lax.fori_loop(0, qi + 1, body, init)
        return acc / l

    n_win = WINDOW // tk + 1

    def win_head(g, i):
        q = qm[g][i]
        carry = (jnp.full((tq, 1), NEG_INF, F32), jnp.zeros((tq, 1), F32), jnp.zeros((tq, LANES), F32))
        for j in range(n_win):
            kt = qi - (n_win - 1) + j
            start = pl.multiple_of(jnp.maximum(kt, 0) * tk, tk)
            key = kt * tk + key_in_tile
            ok = (key <= t_col) & (key > t_col - WINDOW) & (key >= 0)
            s = jnp.where(ok, _dot_nt(q, kw_ref[pl.ds(start, tk), :]), NEG_INF)
            carry = _online_step(s, vw_ref[pl.ds(start, tk), :], *carry)
        _, l, acc = carry
        return acc / l

    gates = jax.nn.sigmoid(gl_ref[...])
    for i in range(NSA_HPG):
        outs = []
        for g in range(NSA_GROUPS):
            hd = g * NSA_HPG + i
            o = (gates[:, 3 * hd:3 * hd + 1] * o_cmp[g][i]
                 + gates[:, 3 * hd + 1:3 * hd + 2] * sel_head(g, i)
                 + gates[:, 3 * hd + 2:3 * hd + 3] * win_head(g, i))
            outs.append(o)
        o_ref[:, i * LANES:(i + 1) * LANES] = jnp.where(half[0], outs[0], outs[1]).astype(o_ref.dtype)


def _nsa(qn, kcmp, vcmp, ks, vs, kw, vw, gl, ct, e_mat, batch, seq, tq=256):
    nq = seq // tq
    n_cmp_rows = seq // CMP_STRIDE
    n_slc = seq // SEL_BLOCK
    row_blk = lambda w: pl.BlockSpec((tq, w), lambda b, q: (b * nq + q, 0))
    seq_blk = pl.BlockSpec((seq, LANES), lambda b, q: (b, 0))
    cmp_blk = pl.BlockSpec((None, n_cmp_rows, LANES), lambda b, q: (b, 0, 0))
    return pl.pallas_call(
        functools.partial(_nsa_kernel, tq=tq, n_cmp_rows=n_cmp_rows, n_slc=n_slc),
        grid=(batch, nq),
        in_specs=[row_blk(NSA_Q_W), cmp_blk, cmp_blk, seq_blk, seq_blk, seq_blk, seq_blk, row_blk(LANES),
                  _resident(ct.shape), _resident(e_mat.shape)],
        out_specs=row_blk(NSA_Q_W),
        out_shape=jax.ShapeDtypeStruct((batch * seq, NSA_Q_W), BF16),
        scratch_shapes=[pltpu.VMEM((NSA_GROUPS, nq, tq, tq), F32)],
        compiler_params=_params(("parallel", "arbitrary")),
        name="nsa_attention",
    )(qn, kcmp, vcmp, ks, vs, kw, vw, gl, ct, e_mat)


def _diff_kernel(q_ref, k_ref, v_ref, lam_ref, sub_ref, o_ref, *, tq, lam_init):
    qi = pl.program_id(2)
    q0 = qi * tq
    tk = tq
    lane = lax.broadcasted_iota(jnp.int32, (1, LANES), 1)
    t_col = q0 + lax.broadcasted_iota(jnp.int32, (tq, 1), 0)
    key_in_tile = lax.broadcasted_iota(jnp.int32, (1, tk), 1)
    q = q_ref[...]
    zero = jnp.zeros((), BF16)
    qs = [jnp.where(lane < HEAD_DIM, q, zero), jnp.where(lane >= HEAD_DIM, q, zero)]

    def tile(kt, carry, masked):
        start = pl.multiple_of(kt * tk, tk)
        k = k_ref[pl.ds(start, tk), :]
        v = v_ref[pl.ds(start, tk), :]
        out = []
        for a in range(2):
            s = _dot_nt(qs[a], k)
            if masked:
                s = jnp.where((kt * tk + key_in_tile) <= t_col, s, NEG_INF)
            out.extend(_online_step(s, v, *carry[3 * a:3 * a + 3]))
        return tuple(out)

    init = (jnp.full((tq, 1), NEG_INF, F32), jnp.zeros((tq, 1), F32), jnp.zeros((tq, LANES), F32)) * 2
    carry = lax.fori_loop(0, qi, lambda kt, c: tile(kt, c, False), init)
    _, l1, a1, _, l2, a2 = tile(qi, carry, True)

    lam_rows = lam_ref[...]
    lam = (jnp.exp(jnp.sum(lam_rows[0:1] * lam_rows[1:2], axis=-1, keepdims=True))
           - jnp.exp(jnp.sum(lam_rows[2:3] * lam_rows[3:4], axis=-1, keepdims=True)) + lam_init)
    o = a1 / l1 - lam * (a2 / l2)
    o_ref[...] = (_rms(o, sub_ref[...]) * (1.0 - lam_init)).astype(o_ref.dtype)


def _diff(dq, dk, dv, lam_rows, subln, lam_init, batch, seq, tq=256):
    nq = seq // tq
    return pl.pallas_call(
        functools.partial(_diff_kernel, tq=tq, lam_init=lam_init),
        grid=(batch, DIFF_HEADS, nq),
        in_specs=[pl.BlockSpec((tq, LANES), lambda b, h, q: (b * nq + q, h)),
                  pl.BlockSpec((seq, LANES), lambda b, h, q: (b, h)),
                  pl.BlockSpec((seq, LANES), lambda b, h, q: (b, h)),
                  _resident(lam_rows.shape), _resident(subln.shape)],
        out_specs=pl.BlockSpec((tq, LANES), lambda b, h, q: (b * nq + q, h)),
        out_shape=jax.ShapeDtypeStruct((batch * seq, DIFF_V_W), BF16),
        compiler_params=_params(("parallel", "parallel", "arbitrary")),
        name="diff_attention",
    )(dq, dk, dv, lam_rows, subln)


def _sb_kernel(q_ref, k_ref, v_ref, u_ref, o_ref, *, tq):
    qi = pl.program_id(2)
    tk = tq
    lane = lax.broadcasted_iota(jnp.int32, (1, LANES), 1)
    row = lax.broadcasted_iota(jnp.int32, (tq, 1), 0)
    key_in_tile = lax.broadcasted_iota(jnp.int32, (1, tk), 1)
    strict = key_in_tile < row
    q = q_ref[...]
    zero = jnp.zeros((), BF16)
    u = u_ref[...]

    def tile(kt, qa, c, acc, diagonal):
        start = pl.multiple_of(kt * tk, tk)
        z = _dot_nt(qa, k_ref[pl.ds(start, tk), :])
        soft = jnp.log(1.0 + jnp.exp(-jnp.abs(z)))
        log_rest = jnp.minimum(-z, 0.0) - soft
        log_beta = jnp.minimum(z, 0.0) - soft
        if diagonal:
            log_rest = jnp.where(strict, log_rest, 0.0)
        hi = log_rest.astype(BF16)
        lo = (log_rest - hi.astype(F32)).astype(BF16)
        suffix = _dot(hi, u) + _dot(lo, u) + c
        a = jnp.exp(log_beta + suffix)
        if diagonal:
            a = jnp.where(strict, a, 0.0)
        acc = acc + _dot(a.astype(BF16), v_ref[pl.ds(start, tk), :])
        c = c + jnp.sum(log_rest, axis=-1, keepdims=True)
        return c, acc

    outs = []
    for sel in (lane < HEAD_DIM, lane >= HEAD_DIM):
        qa = jnp.where(sel, q, zero)
        c, acc = tile(qi, qa, jnp.zeros((tq, 1), F32), jnp.zeros((tq, LANES), F32), True)

        def body(r, carry, qa=qa):
            return tile(qi - 1 - r, qa, carry[0], carry[1], False)

        _, acc = lax.fori_loop(0, qi, body, (c, acc))
        outs.append(acc)
    o_ref[...] = jnp.where(lane < HEAD_DIM, outs[0], outs[1]).astype(o_ref.dtype)


def _sb(qkv, u_mat, batch, seq, tq=256):
    nq = seq // tq
    n_pairs = SB_HEADS // 2
    return pl.pallas_call(
        functools.partial(_sb_kernel, tq=tq),
        grid=(batch, n_pairs, nq),
        in_specs=[pl.BlockSpec((tq, LANES), lambda b, p, q: (b * nq + q, p)),
                  pl.BlockSpec((seq, LANES), lambda b, p, q: (b, n_pairs + p)),
                  pl.BlockSpec((seq, LANES), lambda b, p, q: (b, 2 * n_pairs + p)),
                  _resident(u_mat.shape)],
        out_specs=pl.BlockSpec((tq, LANES), lambda b, p, q: (b * nq + q, p)),
        out_shape=jax.ShapeDtypeStruct((batch * seq, D_MODEL), BF16),
        compiler_params=_params(("parallel", "parallel", "arbitrary")),
        name="sb_attention",
    )(qkv, qkv, qkv, u_mat)


def _mix_ffn_kernel(x_ref, oa_ref, ob_ref, wo_ref, g_ref, wg_ref, wu_ref, wd_ref, gf_ref, out_ref, *,
                    final_norm):
    half = oa_ref.shape[1]
    x = x_ref[...] + _dot(oa_ref[...], wo_ref[0:half, :]) + _dot(ob_ref[...], wo_ref[half:2 * half, :])
    h = _rms(x, g_ref[...]).astype(BF16)
    gate = _dot(h, wg_ref[...])
    up = _dot(h, wu_ref[...])
    act = (gate * jax.nn.sigmoid(gate) * up).astype(BF16)
    acc = x + _dot(act, wd_ref[...])
    if final_norm:
        acc = _rms(acc, gf_ref[...])
    out_ref[...] = acc


def _mix_ffn(x2, oa, ob, oa_blk, ob_blk, wo, g, wg, wu, wd, gf, final_norm, tm=512):
    T = x2.shape[0]
    half = D_MODEL // 2
    return pl.pallas_call(
        functools.partial(_mix_ffn_kernel, final_norm=final_norm),
        grid=(T // tm,),
        in_specs=[pl.BlockSpec((tm, D_MODEL), lambda m: (m, 0)),
                  pl.BlockSpec((tm, half), lambda m: (m, oa_blk)),
                  pl.BlockSpec((tm, half), lambda m: (m, ob_blk)),
                  _resident(wo.shape), _resident((1, D_MODEL)),
                  _resident(wg.shape), _resident(wu.shape), _resident(wd.shape), _resident((1, D_MODEL))],
        out_specs=pl.BlockSpec((tm, D_MODEL), lambda m: (m, 0)),
        out_shape=jax.ShapeDtypeStruct((T, D_MODEL), F32),
        compiler_params=_params(("parallel",)),
        name="outproj_ffn",
    )(x2, oa, ob, wo, g.reshape(1, D_MODEL), wg, wu, wd, gf.reshape(1, D_MODEL))


def _even_in_columns():
    off = np.cumsum([0, NSA_Q_W, NSA_KV_W, NSA_KV_W, NSA_KV_W, NSA_KV_W, NSA_KV_W, NSA_KV_W,
                     NSA_GATE_W, DIFF_QK_W, DIFF_QK_W, DIFF_V_W])
    nq, kc, vc, ks, vs, kw, vw, gl, dq, dk, dv = off[:-1]
    rng = lambda s, w: np.arange(s, s + w)
    q_cols = np.concatenate([np.concatenate([rng(nq + HEAD_DIM * i, HEAD_DIM),
                                             rng(nq + HEAD_DIM * (NSA_HPG + i), HEAD_DIM)])
                             for i in range(NSA_HPG)])
    pieces = [(q_cols, True), (rng(kc, NSA_KV_W), True), (rng(ks, NSA_KV_W), True), (rng(kw, NSA_KV_W), True),
              (rng(dq, DIFF_QK_W), True), (rng(dk, DIFF_QK_W), True),
              (rng(vc, NSA_KV_W), False), (rng(vs, NSA_KV_W), False), (rng(vw, NSA_KV_W), False),
              (rng(dv, DIFF_V_W), False), (rng(gl, NSA_GATE_W), False)]
    return pieces


def _prep_even_w_in(w):
    pieces = _even_in_columns()
    scale = HEAD_DIM ** -0.5
    cols, widths = [], []
    for idx, (src, _) in enumerate(pieces):
        blk = w[:, src]
        if idx in (0, 4):
            blk = blk * scale
        pad = (-blk.shape[1]) % LANES
        if pad:
            blk = jnp.pad(blk, ((0, 0), (0, pad)))
        cols.append(blk)
        widths.append(blk.shape[1])
    n_rope = sum(1 for _, r in pieces if r)
    return jnp.concatenate(cols, axis=1).astype(BF16), widths, n_rope


def _nsa_out_rows():
    rows = []
    for i in range(NSA_HPG):
        for g in range(NSA_GROUPS):
            hd = g * NSA_HPG + i
            rows.append(np.arange(hd * HEAD_DIM, (hd + 1) * HEAD_DIM))
    return np.concatenate(rows + [np.arange(NSA_Q_W, NSA_Q_W + DIFF_V_W)])


def _prep_cmp(pos, w):
    half = CMP_BLOCK // 2
    p2 = jnp.tile(pos.reshape(2, half, 1, HEAD_DIM), (1, 1, NSA_GROUPS, 1)).reshape(2, half * LANES)
    w4 = w.reshape(2, half, HEAD_DIM, HEAD_DIM)
    wbd = jnp.zeros((2, half, NSA_GROUPS, HEAD_DIM, NSA_GROUPS, HEAD_DIM), w.dtype)
    for g in range(NSA_GROUPS):
        wbd = wbd.at[:, :, g, :, g, :].set(w4)
    return p2.astype(F32), wbd.reshape(2, half * LANES, LANES).astype(BF16)


def _rope_tables(seq):
    inv = 1.0 / (ROPE_THETA ** (jnp.arange(0, HEAD_DIM, 2, dtype=F32) / HEAD_DIM))
    ang = jnp.arange(seq, dtype=F32)[:, None] * inv[None, :]
    cos, sin = jnp.cos(ang), jnp.sin(ang)
    reps = LANES // HEAD_DIM
    cos_t = jnp.tile(jnp.concatenate([cos, cos], axis=1), (1, reps))
    sin_t = jnp.tile(jnp.concatenate([-sin, sin], axis=1), (1, reps))
    return cos_t, sin_t


def _selection_constants(seq, tq):
    n_cmp_rows = seq // CMP_STRIDE
    n_slc = seq // SEL_BLOCK
    n = np.arange(n_cmp_rows)[None, :] * CMP_STRIDE
    j = np.arange(n_slc)[:, None] * SEL_BLOCK
    ct = ((n < j + SEL_BLOCK) & (n + CMP_BLOCK > j)).astype(np.float32)
    e = np.zeros((LANES, seq), np.float32)
    e[np.arange(seq) // SEL_BLOCK, np.arange(seq)] = 1.0
    u = (np.arange(tq)[:, None] > np.arange(tq)[None, :]).astype(np.float32)
    return jnp.asarray(ct, BF16), jnp.asarray(e, BF16), jnp.asarray(u, BF16)


def kernel(x, norm_mix, norm_ffn, norm_final, even_w_in, even_w_out, cmp_pos_k, cmp_w_k, cmp_pos_v, cmp_w_v,
           diff_lq1, diff_lk1, diff_lq2, diff_lk2, diff_subln, odd_w_in, odd_w_out, ffn_w_gate, ffn_w_up,
           ffn_w_down):
    batch, seq, _ = x.shape
    tq = 256
    assert seq % tq == 0 and seq // SEL_BLOCK <= LANES and seq // SEL_BLOCK >= SEL_TOPK
    cos_t, sin_t = _rope_tables(seq)
    ct, e_mat, u_mat = _selection_constants(seq, tq)
    out_rows = _nsa_out_rows()
    sb_scale = jnp.concatenate([jnp.full((D_MODEL,), HEAD_DIM ** -0.5, F32), jnp.ones((2 * D_MODEL,), F32)])

    x2 = x.reshape(batch * seq, D_MODEL)
    for layer in range(DEPTH):
        last = layer == DEPTH - 1
        if layer % 2 == 0:
            e = layer // 2
            lam_init = 0.8 - 0.6 * math.exp(-0.3 * layer)
            w_in, widths, n_rope = _prep_even_w_in(even_w_in[e])
            qn, kc, ks, kw, dq, dk, vc, vs, vw, dv, gl = _inproj(
                x2, norm_mix[layer], w_in, cos_t, sin_t, widths, n_rope, True, seq)
            pk, wk = _prep_cmp(cmp_pos_k[e], cmp_w_k[e])
            pv, wv = _prep_cmp(cmp_pos_v[e], cmp_w_v[e])
            kcmp, vcmp = _compress(kc, vc, pk, pv, wk, wv, batch, seq)
            o_nsa = _nsa(qn, kcmp, vcmp, ks, vs, kw, vw, gl, ct, e_mat, batch, seq, tq)
            lam_rows = jnp.zeros((8, LANES), F32).at[0:4, 0:HEAD_DIM].set(
                jnp.stack([diff_lq1[e], diff_lk1[e], diff_lq2[e], diff_lk2[e]]))
            o_diff = _diff(dq, dk, dv, lam_rows, diff_subln[e].reshape(1, DIFF_VDIM), lam_init, batch, seq, tq)
            wo = even_w_out[e][out_rows].astype(BF16)
            oa, ob, oa_blk, ob_blk = o_nsa, o_diff, 0, 0
        else:
            o = layer // 2
            w_in = (odd_w_in[o] * sb_scale[None, :]).astype(BF16)
            (qkv,) = _inproj(x2, norm_mix[layer], w_in, cos_t, sin_t, [3 * D_MODEL], 0, False, seq)
            o_sb = _sb(qkv, u_mat, batch, seq, tq)
            wo = odd_w_out[o].astype(BF16)
            oa, ob, oa_blk, ob_blk = o_sb, o_sb, 0, 1
        x2 = _mix_ffn(x2, oa, ob, oa_blk, ob_blk, wo, norm_ffn[layer],
                      ffn_w_gate[layer].astype(BF16), ffn_w_up[layer].astype(BF16),
                      ffn_w_down[layer].astype(BF16), norm_final, last)
    return x2.reshape(batch, seq, D_MODEL)
```

```python
import functools
import math

import numpy as np
import jax
import jax.numpy as jnp
from jax import lax
from jax.experimental import pallas as pl
from jax.experimental.pallas import tpu as pltpu

D_MODEL = 1024
DEPTH = 4
HEAD_DIM = 64
ROPE_THETA = 10000.0
NORM_EPS = 1e-6
NEG_INF = -1e30
BIG = 1e30

NSA_HEADS = 8
NSA_GROUPS = 2
NSA_HPG = NSA_HEADS // NSA_GROUPS
CMP_BLOCK = 32
CMP_STRIDE = 16
SEL_BLOCK = 64
SEL_TOPK = 16
WINDOW = 512

DIFF_HEADS = 4
DIFF_VDIM = 2 * HEAD_DIM

SB_HEADS = D_MODEL // HEAD_DIM
FFN_HIDDEN = -(-(8 * D_MODEL) // (3 * 256)) * 256

NSA_Q_W = NSA_HEADS * HEAD_DIM
NSA_KV_W = NSA_GROUPS * HEAD_DIM
NSA_GATE_W = 3 * NSA_HEADS
DIFF_QK_W = DIFF_HEADS * 2 * HEAD_DIM
DIFF_V_W = DIFF_HEADS * DIFF_VDIM

LANES = 128
SUBLANES = 8
VMEM_LIMIT = 56 * 1024 * 1024

BF16 = jnp.bfloat16
F32 = jnp.float32


def _dot(a, b):
    return jnp.dot(a, b, preferred_element_type=F32)


def _dot_nt(a, b):
    return lax.dot_general(a, b, (((1,), (1,)), ((), ())), preferred_element_type=F32)


def _rms(x, g):
    return x * lax.rsqrt(jnp.mean(x * x, axis=-1, keepdims=True) + NORM_EPS) * g


def _params(sem, vmem=VMEM_LIMIT):
    return pltpu.CompilerParams(dimension_semantics=sem, vmem_limit_bytes=vmem)


def _resident(shape):
    nd = len(shape)
    return pl.BlockSpec(shape, lambda *_: (0,) * nd, pipeline_mode=pl.Buffered(1))


def _inproj_kernel(x_ref, g_ref, w_ref, cos_ref, sin_ref, *out_refs, widths, n_rope, gate_last):
    h = _rms(x_ref[...], g_ref[...]).astype(BF16)
    lane = lax.broadcasted_iota(jnp.int32, (1, LANES), 1)
    first_half = (lane % HEAD_DIM) < (HEAD_DIM // 2)
    col = 0
    for oi, (o_ref, w) in enumerate(zip(out_refs, widths)):
        y = _dot(h, w_ref[:, col:col + w])
        if oi < n_rope:
            cos, sin = cos_ref[...], sin_ref[...]
            for b in range(w // LANES):
                yb = y[:, b * LANES:(b + 1) * LANES]
                partner = jnp.where(first_half, pltpu.roll(yb, LANES - HEAD_DIM // 2, 1),
                                    pltpu.roll(yb, HEAD_DIM // 2, 1))
                o_ref[:, b * LANES:(b + 1) * LANES] = (yb * cos + partner * sin).astype(o_ref.dtype)
        else:
            o_ref[...] = y.astype(o_ref.dtype)
        col += w


def _inproj(x2, g, w, cos_t, sin_t, widths, n_rope, gate_last, seq, tm=512):
    T = x2.shape[0]
    n_pos_blocks = seq // tm
    out_shape = [jax.ShapeDtypeStruct((T, wd), F32 if (gate_last and i == len(widths) - 1) else BF16)
                 for i, wd in enumerate(widths)]
    out_specs = [pl.BlockSpec((tm, wd), lambda m: (m, 0)) for wd in widths]
    return pl.pallas_call(
        functools.partial(_inproj_kernel, widths=tuple(widths), n_rope=n_rope, gate_last=gate_last),
        grid=(T // tm,),
        in_specs=[pl.BlockSpec((tm, D_MODEL), lambda m: (m, 0)),
                  _resident((1, D_MODEL)),
                  _resident(w.shape),
                  pl.BlockSpec((tm, LANES), lambda m: (m % n_pos_blocks, 0)),
                  pl.BlockSpec((tm, LANES), lambda m: (m % n_pos_blocks, 0))],
        out_specs=out_specs,
        out_shape=out_shape,
        compiler_params=_params(("parallel",)),
        name="inproj",
    )(x2, g.reshape(1, D_MODEL), w, cos_t, sin_t)


def _compress_kernel(kc_ref, vc_ref, pk_ref, pv_ref, wk_ref, wv_ref, ko_ref, vo_ref):
    def one(x_ref, p_ref, w_ref, o_ref):
        x = x_ref[...].astype(F32)
        first = _dot((x + p_ref[0:1, :]).astype(BF16), w_ref[0])
        second = _dot((x + p_ref[1:2, :]).astype(BF16), w_ref[1])
        n = second.shape[0]
        o_ref[...] = (first + pltpu.roll(second, n - 1, 0)).astype(o_ref.dtype)

    one(kc_ref, pk_ref, wk_ref, ko_ref)
    one(vc_ref, pv_ref, wv_ref, vo_ref)


def _compress(kc, vc, pk, pv, wk, wv, batch, seq):
    rows = seq // CMP_STRIDE
    width = CMP_STRIDE * LANES
    kc3 = kc.reshape(batch, rows, width)
    vc3 = vc.reshape(batch, rows, width)
    blk = pl.BlockSpec((None, rows, width), lambda b: (b, 0, 0))
    oblk = pl.BlockSpec((None, rows, LANES), lambda b: (b, 0, 0))
    return pl.pallas_call(
        _compress_kernel,
        grid=(batch,),
        in_specs=[blk, blk, _resident(pk.shape), _resident(pv.shape), _resident(wk.shape), _resident(wv.shape)],
        out_specs=[oblk, oblk],
        out_shape=[jax.ShapeDtypeStruct((batch, rows, LANES), BF16)] * 2,
        compiler_params=_params(("parallel",)),
        name="nsa_compress",
    )(kc3, vc3, pk, pv, wk, wv)


def _online_steps(scores, values, carries):
    m_new = [jnp.maximum(m, jnp.max(s, axis=-1, keepdims=True)) for s, (m, _, _) in zip(scores, carries)]
    p = [jnp.exp(s - mn) for s, mn in zip(scores, m_new)]
    pv = [_dot(pi.astype(BF16), v) for pi, v in zip(p, values)]
    out = []
    for pi, pvi, mn, (m, l, acc) in zip(p, pv, m_new, carries):
        alpha = jnp.exp(m - mn)
        out.append((mn, alpha * l + jnp.sum(pi, axis=-1, keepdims=True), alpha * acc + pvi))
    return out


def _online_init(tq, n):
    return [(jnp.full((tq, 1), NEG_INF, F32), jnp.zeros((tq, 1), F32), jnp.zeros((tq, LANES), F32))
            for _ in range(n)]


def _nsa_kernel(q_ref, kcmp_ref, vcmp_ref, ks_ref, vs_ref, kw_ref, vw_ref, gl_ref, ct_ref, e_ref,
                o_ref, *, tq, tk, n_cmp_rows, n_slc):
    qi = pl.program_id(1)
    q0 = qi * tq
    lane = lax.broadcasted_iota(jnp.int32, (1, LANES), 1)
    half = [lane < HEAD_DIM, lane >= HEAD_DIM]
    t_col = q0 + lax.broadcasted_iota(jnp.int32, (tq, 1), 0)

    qm = [[jnp.where(half[g], q_ref[:, i * LANES:(i + 1) * LANES], jnp.zeros((), BF16))
           for i in range(NSA_HPG)] for g in range(NSA_GROUPS)]

    heads = [(g, i) for g in range(NSA_GROUPS) for i in range(NSA_HPG)]
    n_heads = len(heads)

    n_idx = lax.broadcasted_iota(jnp.int32, (1, n_cmp_rows), 1)
    cmp_ok = (n_idx * CMP_STRIDE + (CMP_BLOCK - 1)) <= t_col
    kcmp = kcmp_ref[...]
    vcmp = vcmp_ref[...]
    s_cmp = [_dot_nt(qm[g][i], kcmp) for g, i in heads]
    p_cmp = []
    for s in s_cmp:
        s = jnp.where(cmp_ok, s, NEG_INF)
        e = jnp.where(cmp_ok, jnp.exp(s - jnp.max(s, axis=-1, keepdims=True)), 0.0)
        l = jnp.sum(e, axis=-1, keepdims=True)
        p_cmp.append(e * jnp.where(l > 0.0, 1.0 / l, 0.0))
    o_cmp = [_dot(p.astype(BF16), vcmp) for p in p_cmp]
    p_sum = [sum(p_cmp[g * NSA_HPG + i] for i in range(NSA_HPG)) for g in range(NSA_GROUPS)]

    n_grp = n_slc // SUBLANES
    jb = lax.broadcasted_iota(jnp.int32, (n_slc, 1), 0)
    sub = lax.broadcasted_iota(jnp.int32, (SUBLANES, 1), 0)
    qblk = (q0 + lax.broadcasted_iota(jnp.int32, (1, tq), 1)) // SEL_BLOCK
    forced = (jb == 0) | (jb == qblk) | (jb == qblk - 1)
    future = jb > qblk
    key_in_tile = lax.broadcasted_iota(jnp.int32, (1, tk), 1)
    ct = ct_ref[...]
    dropped = []
    for g in range(NSA_GROUPS):
        r0 = p_sum[g]
        p_hi = r0.astype(BF16)
        r1 = r0 - p_hi.astype(F32)
        p_mid = r1.astype(BF16)
        p_lo = (r1 - p_mid.astype(F32)).astype(BF16)
        imp = _dot_nt(ct, p_hi) + _dot_nt(ct, p_mid) + _dot_nt(ct, p_lo)
        imp = jnp.where(forced, BIG, jnp.where(future, -BIG, imp))
        imp_g = [imp[r * SUBLANES:(r + 1) * SUBLANES, :] for r in range(n_grp)]
        cnt = [jnp.zeros((SUBLANES, tq), F32) for _ in range(n_grp)]
        for j in range(n_slc):
            r, jj = divmod(j, SUBLANES)
            row = imp_g[r][jj:jj + 1, :]
            for b in range(n_grp):
                if b < r:
                    inc = jnp.where(row > imp_g[b], 1.0, 0.0)
                elif b > r:
                    inc = jnp.where(row >= imp_g[b], 1.0, 0.0)
                else:
                    tie = jnp.where(sub > jj, 1.0, 0.0)
                    inc = jnp.where(row > imp_g[b], 1.0, jnp.where(row == imp_g[b], tie, 0.0))
                cnt[b] = cnt[b] + inc
        drop_t = [jnp.where((cnt[b] < float(SEL_TOPK)) & (imp_g[b] >= 0.0), 0.0, 1.0) for b in range(n_grp)]
        drop_t = jnp.concatenate(drop_t + [jnp.ones((LANES - n_slc, tq), F32)], axis=0)
        dropped.append(drop_t.T.astype(BF16))

    def sel_tile(kt, carry, masked=False):
        start = pl.multiple_of(kt * tk, tk)
        k = ks_ref[pl.ds(start, tk), :]
        v = vs_ref[pl.ds(start, tk), :]
        bias = [_dot(dropped[g], e_ref[kt]) for g in range(NSA_GROUPS)]
        s = [_dot_nt(qm[g][i], k) + bias[g] for g, i in heads]
        if masked:
            ok = (kt * tk + key_in_tile) <= t_col
            s = [jnp.where(ok, x, NEG_INF) for x in s]
        return _online_steps(s, [v] * n_heads, carry)

    n_kt = (q0 + tq + tk - 1) // tk
    carry = lax.fori_loop(0, n_kt - 1, sel_tile, _online_init(tq, n_heads))
    o_sel = [acc / l for _, l, acc in sel_tile(n_kt - 1, carry, True)]

    span = WINDOW + tq
    start = pl.multiple_of(jnp.maximum(q0 - WINDOW, 0), tq)
    key = start + lax.broadcasted_iota(jnp.int32, (1, span), 1)
    ok = (key <= t_col) & (key > t_col - WINDOW)
    k = kw_ref[pl.ds(start, span), :]
    v = vw_ref[pl.ds(start, span), :]
    s_win = [jnp.where(ok, _dot_nt(qm[g][i], k), NEG_INF) for g, i in heads]
    p_win = [jnp.exp(s - jnp.max(s, axis=-1, keepdims=True)) for s in s_win]
    pv_win = [_dot(p.astype(BF16), v) for p in p_win]
    o_win = [pv / jnp.sum(p, axis=-1, keepdims=True) for p, pv in zip(p_win, pv_win)]

    gates = jax.nn.sigmoid(gl_ref[...])
    for i in range(NSA_HPG):
        outs = []
        for g in range(NSA_GROUPS):
            hd = g * NSA_HPG + i
            outs.append(gates[:, 3 * hd:3 * hd + 1] * o_cmp[hd]
                        + gates[:, 3 * hd + 1:3 * hd + 2] * o_sel[hd]
                        + gates[:, 3 * hd + 2:3 * hd + 3] * o_win[hd])
        o_ref[:, i * LANES:(i + 1) * LANES] = jnp.where(half[0], outs[0], outs[1]).astype(o_ref.dtype)


def _nsa(qn, kcmp, vcmp, ks, vs, kw, vw, gl, ct, e_mat, batch, seq, tq, tk):
    nq = seq // tq
    n_cmp_rows = seq // CMP_STRIDE
    n_slc = seq // SEL_BLOCK
    row_blk = lambda w: pl.BlockSpec((tq, w), lambda b, q: (b * nq + q, 0))
    seq_blk = pl.BlockSpec((seq, LANES), lambda b, q: (b, 0))
    cmp_blk = pl.BlockSpec((None, n_cmp_rows, LANES), lambda b, q: (b, 0, 0))
    return pl.pallas_call(
        functools.partial(_nsa_kernel, tq=tq, tk=tk, n_cmp_rows=n_cmp_rows, n_slc=n_slc),
        grid=(batch, nq),
        in_specs=[row_blk(NSA_Q_W), cmp_blk, cmp_blk, seq_blk, seq_blk, seq_blk, seq_blk, row_blk(LANES),
                  _resident(ct.shape), _resident(e_mat.shape)],
        out_specs=row_blk(NSA_Q_W),
        out_shape=jax.ShapeDtypeStruct((batch * seq, NSA_Q_W), BF16),
        compiler_params=_params(("parallel", "arbitrary")),
        name="nsa_attention",
    )(qn, kcmp, vcmp, ks, vs, kw, vw, gl, ct, e_mat)


def _diff_kernel(q_ref, k_ref, v_ref, lam_ref, sub_ref, o_ref, *, tq, tk, n_heads, lam_init):
    qi = pl.program_id(2)
    n_diag = max(tq // tk, 1)
    lane = lax.broadcasted_iota(jnp.int32, (1, LANES), 1)
    row = lax.broadcasted_iota(jnp.int32, (tq, 1), 0)
    key_in_tile = lax.broadcasted_iota(jnp.int32, (1, tk), 1)
    zero = jnp.zeros((), BF16)
    qs = []
    for h in range(n_heads):
        q = q_ref[:, h * LANES:(h + 1) * LANES]
        qs += [jnp.where(lane < HEAD_DIM, q, zero), jnp.where(lane >= HEAD_DIM, q, zero)]

    def tile(kt, carry, masked=False):
        start = pl.multiple_of(kt * tk, tk)
        k = [k_ref[pl.ds(start, tk), h * LANES:(h + 1) * LANES] for h in range(n_heads)]
        v = [v_ref[pl.ds(start, tk), h * LANES:(h + 1) * LANES] for h in range(n_heads)]
        s = [_dot_nt(qs[c], k[c // 2]) for c in range(2 * n_heads)]
        if masked:
            ok = (kt * tk + key_in_tile) <= (qi * tq + row)
            s = [jnp.where(ok, x, NEG_INF) for x in s]
        return _online_steps(s, [v[c // 2] for c in range(2 * n_heads)], carry)

    n_full = (qi * tq) // tk
    carry = lax.fori_loop(0, n_full, tile, _online_init(tq, 2 * n_heads))
    for j in range(n_diag):
        carry = tile(n_full + j, carry, True)

    lam_rows = lam_ref[...]
    lam = (jnp.exp(jnp.sum(lam_rows[0:1] * lam_rows[1:2], axis=-1, keepdims=True))
           - jnp.exp(jnp.sum(lam_rows[2:3] * lam_rows[3:4], axis=-1, keepdims=True)) + lam_init)
    for h in range(n_heads):
        (_, l1, a1), (_, l2, a2) = carry[2 * h], carry[2 * h + 1]
        o = a1 / l1 - lam * (a2 / l2)
        o_ref[:, h * LANES:(h + 1) * LANES] = (_rms(o, sub_ref[...]) * (1.0 - lam_init)).astype(o_ref.dtype)


def _diff(dq, dk, dv, lam_rows, subln, lam_init, batch, seq, tq, tk, n_heads=2):
    nq = seq // tq
    w = n_heads * LANES
    return pl.pallas_call(
        functools.partial(_diff_kernel, tq=tq, tk=tk, n_heads=n_heads, lam_init=lam_init),
        grid=(batch, DIFF_HEADS // n_heads, nq),
        in_specs=[pl.BlockSpec((tq, w), lambda b, h, q: (b * nq + q, h)),
                  pl.BlockSpec((seq, w), lambda b, h, q: (b, h)),
                  pl.BlockSpec((seq, w), lambda b, h, q: (b, h)),
                  _resident(lam_rows.shape), _resident(subln.shape)],
        out_specs=pl.BlockSpec((tq, w), lambda b, h, q: (b * nq + q, h)),
        out_shape=jax.ShapeDtypeStruct((batch * seq, DIFF_V_W), BF16),
        compiler_params=_params(("parallel", "parallel", "arbitrary")),
        name="diff_attention",
    )(dq, dk, dv, lam_rows, subln)


def _sb_kernel(q_ref, k_ref, v_ref, u_ref, o_ref, *, tq, tk):
    qi = pl.program_id(2)
    n_diag = tq // tk
    lane = lax.broadcasted_iota(jnp.int32, (1, LANES), 1)
    row = lax.broadcasted_iota(jnp.int32, (tq, 1), 0)
    key_in_tile = lax.broadcasted_iota(jnp.int32, (1, tk), 1)
    q = q_ref[...]
    zero = jnp.zeros((), BF16)
    qa = [jnp.where(lane < HEAD_DIM, q, zero), jnp.where(lane >= HEAD_DIM, q, zero)]
    u2 = u_ref[...]

    def tile(kt, carry, diag_j=None):
        diagonal = diag_j is not None
        if diagonal:
            strict = (diag_j * tk + key_in_tile) < row
        start = pl.multiple_of(kt * tk, tk)
        k = k_ref[pl.ds(start, tk), :]
        v = v_ref[pl.ds(start, tk), :]
        heads = range(2)
        z = [_dot_nt(qa[a], k) for a in heads]
        nlr, pieces = [], []
        for a in heads:
            soft = jnp.log(1.0 + jnp.exp(-jnp.abs(z[a])))
            n = jnp.maximum(z[a], 0.0) + soft
            if diagonal:
                n = jnp.where(strict, n, 0.0)
            hi = n.astype(BF16)
            lo = (n - hi.astype(F32)).astype(BF16)
            nlr.append(n)
            pieces.append(jnp.concatenate([hi, lo], axis=1))
        nsuf = [_dot(pieces[a], u2) for a in heads]
        w = []
        for a in heads:
            wa = jnp.exp(z[a] - nlr[a] - nsuf[a] - carry[2 * a])
            if diagonal:
                wa = jnp.where(strict, wa, 0.0)
            w.append(wa.astype(BF16))
        pv = [_dot(w[a], v) for a in heads]
        out = []
        for a in heads:
            out += [carry[2 * a] + nsuf[a][:, 0:1] + nlr[a][:, 0:1], carry[2 * a + 1] + pv[a]]
        return tuple(out)

    init = (jnp.zeros((tq, 1), F32), jnp.zeros((tq, LANES), F32)) * 2
    carry = init
    for j in reversed(range(n_diag)):
        carry = tile(qi * n_diag + j, carry, j)
    carry = lax.fori_loop(0, qi * n_diag, lambda r, cr: tile(qi * n_diag - 1 - r, cr), carry)
    o_ref[...] = jnp.where(lane < HEAD_DIM, carry[1], carry[3]).astype(o_ref.dtype)


def _sb(qkv, u_mat, batch, seq, tq, tk):
    nq = seq // tq
    n_pairs = SB_HEADS // 2
    return pl.pallas_call(
        functools.partial(_sb_kernel, tq=tq, tk=tk),
        grid=(batch, n_pairs, nq),
        in_specs=[pl.BlockSpec((tq, LANES), lambda b, p, q: (b * nq + q, p)),
                  pl.BlockSpec((seq, LANES), lambda b, p, q: (b, n_pairs + p)),
                  pl.BlockSpec((seq, LANES), lambda b, p, q: (b, 2 * n_pairs + p)),
                  _resident(u_mat.shape)],
        out_specs=pl.BlockSpec((tq, LANES), lambda b, p, q: (b * nq + q, p)),
        out_shape=jax.ShapeDtypeStruct((batch * seq, D_MODEL), BF16),
        compiler_params=_params(("parallel", "parallel", "arbitrary")),
        name="sb_attention",
    )(qkv, qkv, qkv, u_mat)


def _mix_ffn_kernel(x_ref, oa_ref, ob_ref, wo_ref, g_ref, wg_ref, wu_ref, wd_ref, gf_ref, out_ref, *,
                    final_norm):
    half = oa_ref.shape[1]
    x = x_ref[...] + _dot(oa_ref[...], wo_ref[0:half, :]) + _dot(ob_ref[...], wo_ref[half:2 * half, :])
    h = _rms(x, g_ref[...]).astype(BF16)
    gate = _dot(h, wg_ref[...])
    up = _dot(h, wu_ref[...])
    act = (gate * jax.nn.sigmoid(gate) * up).astype(BF16)
    acc = x + _dot(act, wd_ref[...])
    if final_norm:
        acc = _rms(acc, gf_ref[...])
    out_ref[...] = acc


def _mix_ffn(x2, oa, ob, oa_blk, ob_blk, wo, g, wg, wu, wd, gf, final_norm, tm=512):
    T = x2.shape[0]
    half = D_MODEL // 2
    return pl.pallas_call(
        functools.partial(_mix_ffn_kernel, final_norm=final_norm),
        grid=(T // tm,),
        in_specs=[pl.BlockSpec((tm, D_MODEL), lambda m: (m, 0)),
                  pl.BlockSpec((tm, half), lambda m: (m, oa_blk)),
                  pl.BlockSpec((tm, half), lambda m: (m, ob_blk)),
                  _resident(wo.shape), _resident((1, D_MODEL)),
                  _resident(wg.shape), _resident(wu.shape), _resident(wd.shape), _resident((1, D_MODEL))],
        out_specs=pl.BlockSpec((tm, D_MODEL), lambda m: (m, 0)),
        out_shape=jax.ShapeDtypeStruct((T, D_MODEL), F32),
        compiler_params=_params(("parallel",)),
        name="outproj_ffn",
    )(x2, oa, ob, wo, g.reshape(1, D_MODEL), wg, wu, wd, gf.reshape(1, D_MODEL))


def _even_in_columns():
    off = np.cumsum([0, NSA_Q_W, NSA_KV_W, NSA_KV_W, NSA_KV_W, NSA_KV_W, NSA_KV_W, NSA_KV_W,
                     NSA_GATE_W, DIFF_QK_W, DIFF_QK_W, DIFF_V_W])
    nq, kc, vc, ks, vs, kw, vw, gl, dq, dk, dv = off[:-1]
    rng = lambda s, w: np.arange(s, s + w)
    q_cols = np.concatenate([np.concatenate([rng(nq + HEAD_DIM * i, HEAD_DIM),
                                             rng(nq + HEAD_DIM * (NSA_HPG + i), HEAD_DIM)])
                             for i in range(NSA_HPG)])
    pieces = [(q_cols, True), (rng(kc, NSA_KV_W), True), (rng(ks, NSA_KV_W), True), (rng(kw, NSA_KV_W), True),
              (rng(dq, DIFF_QK_W), True), (rng(dk, DIFF_QK_W), True),
              (rng(vc, NSA_KV_W), False), (rng(vs, NSA_KV_W), False), (rng(vw, NSA_KV_W), False),
              (rng(dv, DIFF_V_W), False), (rng(gl, NSA_GATE_W), False)]
    return pieces


def _prep_even_w_in(w):
    pieces = _even_in_columns()
    scale = HEAD_DIM ** -0.5
    cols, widths = [], []
    for idx, (src, _) in enumerate(pieces):
        blk = w[:, src]
        if idx in (0, 4):
            blk = blk * scale
        pad = (-blk.shape[1]) % LANES
        if pad:
            blk = jnp.pad(blk, ((0, 0), (0, pad)))
        cols.append(blk)
        widths.append(blk.shape[1])
    n_rope = sum(1 for _, r in pieces if r)
    return jnp.concatenate(cols, axis=1).astype(BF16), widths, n_rope


def _nsa_out_rows():
    rows = []
    for i in range(NSA_HPG):
        for g in range(NSA_GROUPS):
            hd = g * NSA_HPG + i
            rows.append(np.arange(hd * HEAD_DIM, (hd + 1) * HEAD_DIM))
    return np.concatenate(rows + [np.arange(NSA_Q_W, NSA_Q_W + DIFF_V_W)])


def _prep_cmp(pos, w):
    half = CMP_BLOCK // 2
    p2 = jnp.tile(pos.reshape(2, half, 1, HEAD_DIM), (1, 1, NSA_GROUPS, 1)).reshape(2, half * LANES)
    w4 = w.reshape(2, half, HEAD_DIM, HEAD_DIM)
    wbd = jnp.zeros((2, half, NSA_GROUPS, HEAD_DIM, NSA_GROUPS, HEAD_DIM), w.dtype)
    for g in range(NSA_GROUPS):
        wbd = wbd.at[:, :, g, :, g, :].set(w4)
    return p2.astype(F32), wbd.reshape(2, half * LANES, LANES).astype(BF16)


def _rope_tables(seq):
    inv = 1.0 / (ROPE_THETA ** (jnp.arange(0, HEAD_DIM, 2, dtype=F32) / HEAD_DIM))
    ang = jnp.arange(seq, dtype=F32)[:, None] * inv[None, :]
    cos, sin = jnp.cos(ang), jnp.sin(ang)
    reps = LANES // HEAD_DIM
    cos_t = jnp.tile(jnp.concatenate([cos, cos], axis=1), (1, reps))
    sin_t = jnp.tile(jnp.concatenate([-sin, sin], axis=1), (1, reps))
    return cos_t, sin_t


def _selection_constants(seq, sb_tk, nsa_tk):
    n_cmp_rows = seq // CMP_STRIDE
    n_slc = seq // SEL_BLOCK
    n = np.arange(n_cmp_rows)[None, :] * CMP_STRIDE
    j = np.arange(n_slc)[:, None] * SEL_BLOCK
    ct = ((n < j + SEL_BLOCK) & (n + CMP_BLOCK > j)).astype(np.float32)
    e = np.zeros((LANES, seq), np.float32)
    e[np.arange(seq) // SEL_BLOCK, np.arange(seq)] = NEG_INF
    e = e.reshape(LANES, seq // nsa_tk, nsa_tk).transpose(1, 0, 2)
    u =(np.arange(sb_tk)[:, None] > np.arange(sb_tk)[None, :]).astype(np.float32)
    return jnp.asarray(ct, BF16), jnp.asarray(e, BF16), jnp.asarray(np.concatenate([u, u], axis=0), BF16)


def kernel(x, norm_mix, norm_ffn, norm_final, even_w_in, even_w_out, cmp_pos_k, cmp_w_k, cmp_pos_v, cmp_w_v,
           diff_lq1, diff_lk1, diff_lq2, diff_lk2, diff_subln, odd_w_in, odd_w_out, ffn_w_gate, ffn_w_up,
           ffn_w_down):
    batch, seq, _ = x.shape
    nsa_tq, nsa_tk = 256, 512
    sb_tq, sb_tk = 512, 256
    diff_tq, diff_tk = 256, 1024
    assert seq % sb_tq == 0 and seq % diff_tk == 0 and seq // SEL_BLOCK <= LANES and seq // SEL_BLOCK >= SEL_TOPK
    cos_t, sin_t = _rope_tables(seq)
    ct, e_mat, u_mat = _selection_constants(seq, sb_tk, nsa_tk)
    out_rows = _nsa_out_rows()
    sb_scale = jnp.concatenate([jnp.full((D_MODEL,), HEAD_DIM ** -0.5, F32), jnp.ones((2 * D_MODEL,), F32)])

    x2 = x.reshape(batch * seq, D_MODEL)
    for layer in range(DEPTH):
        last = layer == DEPTH - 1
        if layer % 2 == 0:
            e = layer // 2
            lam_init = 0.8 - 0.6 * math.exp(-0.3 * layer)
            w_in, widths, n_rope = _prep_even_w_in(even_w_in[e])
            qn, kc, ks, kw, dq, dk, vc, vs, vw, dv, gl = _inproj(
                x2, norm_mix[layer], w_in, cos_t, sin_t, widths, n_rope, True, seq)
            pk, wk = _prep_cmp(cmp_pos_k[e], cmp_w_k[e])
            pv, wv = _prep_cmp(cmp_pos_v[e], cmp_w_v[e])
            kcmp, vcmp = _compress(kc, vc, pk, pv, wk, wv, batch, seq)
            o_nsa = _nsa(qn, kcmp, vcmp, ks, vs, kw, vw, gl, ct, e_mat, batch, seq, nsa_tq, nsa_tk)
            lam_rows = jnp.zeros((8, LANES), F32).at[0:4, 0:HEAD_DIM].set(
                jnp.stack([diff_lq1[e], diff_lk1[e], diff_lq2[e], diff_lk2[e]]))
            o_diff = _diff(dq, dk, dv, lam_rows, diff_subln[e].reshape(1, DIFF_VDIM), lam_init, batch, seq,
                           diff_tq, diff_tk)
            wo = even_w_out[e][out_rows].astype(BF16)
            oa, ob, oa_blk, ob_blk = o_nsa, o_diff, 0, 0
        else:
            o = layer // 2
            w_in = (odd_w_in[o] * sb_scale[None, :]).astype(BF16)
            (qkv,) = _inproj(x2, norm_mix[layer], w_in, cos_t, sin_t, [3 * D_MODEL], 0, False, seq)
            o_sb = _sb(qkv, u_mat, batch, seq, sb_tq, sb_tk)
            wo = odd_w_out[o].astype(BF16)
            oa, ob, oa_blk, ob_blk = o_sb, o_sb, 0, 1
        x2 = _mix_ffn(x2, oa, ob, oa_blk, ob_blk, wo, norm_ffn[layer],
                      ffn_w_gate[layer].astype(BF16), ffn_w_up[layer].astype(BF16),
                      ffn_w_down[layer].astype(BF16), norm_final, last)
    return x2.reshape(batch, seq, D_MODEL)
```

```python
import functools
import math

import numpy as np
import jax
import jax.numpy as jnp
from jax import lax
from jax.experimental import pallas as pl
from jax.experimental.pallas import tpu as pltpu

D_MODEL = 1024
DEPTH = 4
HEAD_DIM = 64
ROPE_THETA = 10000.0
NORM_EPS = 1e-6
NEG_INF = -1e30
BIG = 1e30
SB_UNDERFLOW = 110.0

NSA_HEADS = 8
NSA_GROUPS = 2
NSA_HPG = NSA_HEADS // NSA_GROUPS
CMP_BLOCK = 32
CMP_STRIDE = 16
SEL_BLOCK = 64
SEL_TOPK = 16
WINDOW = 512

DIFF_HEADS = 4
DIFF_VDIM = 2 * HEAD_DIM

SB_HEADS = D_MODEL // HEAD_DIM
FFN_HIDDEN = -(-(8 * D_MODEL) // (3 * 256)) * 256

NSA_Q_W = NSA_HEADS * HEAD_DIM
NSA_KV_W = NSA_GROUPS * HEAD_DIM
NSA_GATE_W = 3 * NSA_HEADS
DIFF_QK_W = DIFF_HEADS * 2 * HEAD_DIM
DIFF_V_W = DIFF_HEADS * DIFF_VDIM

LANES = 128
SUBLANES = 8
VMEM_LIMIT = 56 * 1024 * 1024

BF16 = jnp.bfloat16
F32 = jnp.float32


def _dot(a, b):
    return jnp.dot(a, b, preferred_element_type=F32)


def _dot_nt(a, b):
    return lax.dot_general(a, b, (((1,), (1,)), ((), ())), preferred_element_type=F32)


def _rms(x, g):
    return x * lax.rsqrt(jnp.mean(x * x, axis=-1, keepdims=True) + NORM_EPS) * g


def _params(sem, vmem=VMEM_LIMIT):
    return pltpu.CompilerParams(dimension_semantics=sem, vmem_limit_bytes=vmem)


def _resident(shape):
    nd = len(shape)
    return pl.BlockSpec(shape, lambda *_: (0,) * nd, pipeline_mode=pl.Buffered(1))


def _inproj_kernel(x_ref, g_ref, w_ref, cos_ref, sin_ref, *out_refs, widths, n_rope, gate_last):
    h = _rms(x_ref[...], g_ref[...]).astype(BF16)
    lane = lax.broadcasted_iota(jnp.int32, (1, LANES), 1)
    first_half = (lane % HEAD_DIM) < (HEAD_DIM // 2)
    col = 0
    for oi, (o_ref, w) in enumerate(zip(out_refs, widths)):
        y = _dot(h, w_ref[:, col:col + w])
        if oi < n_rope:
            cos, sin = cos_ref[...], sin_ref[...]
            for b in range(w // LANES):
                yb = y[:, b * LANES:(b + 1) * LANES]
                partner = jnp.where(first_half, pltpu.roll(yb, LANES - HEAD_DIM // 2, 1),
                                    pltpu.roll(yb, HEAD_DIM // 2, 1))
                o_ref[:, b * LANES:(b + 1) * LANES] = (yb * cos + partner * sin).astype(o_ref.dtype)
        else:
            o_ref[...] = y.astype(o_ref.dtype)
        col += w


def _inproj(x2, g, w, cos_t, sin_t, widths, n_rope, gate_last, seq, tm=512):
    T = x2.shape[0]
    n_pos_blocks = seq // tm
    out_shape = [jax.ShapeDtypeStruct((T, wd), F32 if (gate_last and i == len(widths) - 1) else BF16)
                 for i, wd in enumerate(widths)]
    out_specs = [pl.BlockSpec((tm, wd), lambda m: (m, 0)) for wd in widths]
    return pl.pallas_call(
        functools.partial(_inproj_kernel, widths=tuple(widths), n_rope=n_rope, gate_last=gate_last),
        grid=(T // tm,),
        in_specs=[pl.BlockSpec((tm, D_MODEL), lambda m: (m, 0)),
                  _resident((1, D_MODEL)),
                  _resident(w.shape),
                  pl.BlockSpec((tm, LANES), lambda m: (m % n_pos_blocks, 0)),
                  pl.BlockSpec((tm, LANES), lambda m: (m % n_pos_blocks, 0))],
        out_specs=out_specs,
        out_shape=out_shape,
        compiler_params=_params(("parallel",)),
        name="inproj",
    )(x2, g.reshape(1, D_MODEL), w, cos_t, sin_t)


def _compress_kernel(kc_ref, vc_ref, pk_ref, pv_ref, wk_ref, wv_ref, ko_ref, vo_ref):
    def one(x_ref, p_ref, w_ref, o_ref):
        x = x_ref[...].astype(F32)
        first = _dot((x + p_ref[0:1, :]).astype(BF16), w_ref[0])
        second = _dot((x + p_ref[1:2, :]).astype(BF16), w_ref[1])
        n = second.shape[0]
        o_ref[...] = (first + pltpu.roll(second, n - 1, 0)).astype(o_ref.dtype)

    one(kc_ref, pk_ref, wk_ref, ko_ref)
    one(vc_ref, pv_ref, wv_ref, vo_ref)


def _compress(kc, vc, pk, pv, wk, wv, batch, seq):
    rows = seq // CMP_STRIDE
    width = CMP_STRIDE * LANES
    kc3 = kc.reshape(batch, rows, width)
    vc3 = vc.reshape(batch, rows, width)
    blk = pl.BlockSpec((None, rows, width), lambda b: (b, 0, 0))
    oblk = pl.BlockSpec((None, rows, LANES), lambda b: (b, 0, 0))
    return pl.pallas_call(
        _compress_kernel,
        grid=(batch,),
        in_specs=[blk, blk, _resident(pk.shape), _resident(pv.shape), _resident(wk.shape), _resident(wv.shape)],
        out_specs=[oblk, oblk],
        out_shape=[jax.ShapeDtypeStruct((batch, rows, LANES), BF16)] * 2,
        compiler_params=_params(("parallel",)),
        name="nsa_compress",
    )(kc3, vc3, pk, pv, wk, wv)


def _online_steps(scores, values, carries):
    m_new = [jnp.maximum(m, jnp.max(s, axis=-1, keepdims=True)) for s, (m, _, _) in zip(scores, carries)]
    p = [jnp.exp(s - mn) for s, mn in zip(scores, m_new)]
    pv = [_dot(pi.astype(BF16), v) for pi, v in zip(p, values)]
    out = []
    for pi, pvi, mn, (m, l, acc) in zip(p, pv, m_new, carries):
        alpha = jnp.exp(m - mn)
        out.append((mn, alpha * l + jnp.sum(pi, axis=-1, keepdims=True), alpha * acc + pvi))
    return out


def _online_init(tq, n):
    return [(jnp.full((tq, 1), NEG_INF, F32), jnp.zeros((tq, 1), F32), jnp.zeros((tq, LANES), F32))
            for _ in range(n)]


def _nsa_kernel(q_ref, kcmp_ref, vcmp_ref, ks_ref, vs_ref, kw_ref, vw_ref, gl_ref, ct_ref, e_ref,
                o_ref, *, tq, tk, n_cmp_rows, n_slc):
    qi = pl.program_id(1)
    q0 = qi * tq
    lane = lax.broadcasted_iota(jnp.int32, (1, LANES), 1)
    half = [lane < HEAD_DIM, lane >= HEAD_DIM]
    t_col = q0 + lax.broadcasted_iota(jnp.int32, (tq, 1), 0)

    qm = [[jnp.where(half[g], q_ref[:, i * LANES:(i + 1) * LANES], jnp.zeros((), BF16))
           for i in range(NSA_HPG)] for g in range(NSA_GROUPS)]

    heads = [(g, i) for g in range(NSA_GROUPS) for i in range(NSA_HPG)]
    n_heads = len(heads)

    n_idx = lax.broadcasted_iota(jnp.int32, (1, n_cmp_rows), 1)
    cmp_ok = (n_idx * CMP_STRIDE + (CMP_BLOCK - 1)) <= t_col
    kcmp = kcmp_ref[...]
    vcmp = vcmp_ref[...]
    s_cmp = [_dot_nt(qm[g][i], kcmp) for g, i in heads]
    p_cmp = []
    for s in s_cmp:
        s = jnp.where(cmp_ok, s, NEG_INF)
        e = jnp.where(cmp_ok, jnp.exp(s - jnp.max(s, axis=-1, keepdims=True)), 0.0)
        l = jnp.sum(e, axis=-1, keepdims=True)
        p_cmp.append(e * jnp.where(l > 0.0, 1.0 / l, 0.0))
    o_cmp = [_dot(p.astype(BF16), vcmp) for p in p_cmp]
    p_sum = [sum(p_cmp[g * NSA_HPG + i] for i in range(NSA_HPG)) for g in range(NSA_GROUPS)]

    n_grp = n_slc // SUBLANES
    jb = lax.broadcasted_iota(jnp.int32, (n_slc, 1), 0)
    sub = lax.broadcasted_iota(jnp.int32, (SUBLANES, 1), 0)
    qblk = (q0 + lax.broadcasted_iota(jnp.int32, (1, tq), 1)) // SEL_BLOCK
    forced = (jb == 0) | (jb == qblk) | (jb == qblk - 1)
    future = jb > qblk
    key_in_tile = lax.broadcasted_iota(jnp.int32, (1, tk), 1)
    ct = ct_ref[...]
    dropped = []
    for g in range(NSA_GROUPS):
        r0 = p_sum[g]
        p_hi = r0.astype(BF16)
        r1 = r0 - p_hi.astype(F32)
        p_mid = r1.astype(BF16)
        p_lo = (r1 - p_mid.astype(F32)).astype(BF16)
        imp = _dot_nt(ct, p_hi) + _dot_nt(ct, p_mid) + _dot_nt(ct, p_lo)
        imp = jnp.where(forced, BIG, jnp.where(future, -BIG, imp))
        imp_g = [imp[r * SUBLANES:(r + 1) * SUBLANES, :] for r in range(n_grp)]
        cnt = [jnp.zeros((SUBLANES, tq), F32) for _ in range(n_grp)]
        for j in range(n_slc):
            r, jj = divmod(j, SUBLANES)
            row = imp_g[r][jj:jj + 1, :]
            for b in range(n_grp):
                if b < r:
                    inc = jnp.where(row > imp_g[b], 1.0, 0.0)
                elif b > r:
                    inc = jnp.where(row >= imp_g[b], 1.0, 0.0)
                else:
                    tie = jnp.where(sub > jj, 1.0, 0.0)
                    inc = jnp.where(row > imp_g[b], 1.0, jnp.where(row == imp_g[b], tie, 0.0))
                cnt[b] = cnt[b] + inc
        drop_t = [jnp.where((cnt[b] < float(SEL_TOPK)) & (imp_g[b] >= 0.0), 0.0, 1.0) for b in range(n_grp)]
        drop_t = jnp.concatenate(drop_t + [jnp.ones((LANES - n_slc, tq), F32)], axis=0)
        dropped.append(drop_t.T.astype(BF16))

    def sel_tile(kt, carry, masked=False):
        start = pl.multiple_of(kt * tk, tk)
        k = ks_ref[pl.ds(start, tk), :]
        v = vs_ref[pl.ds(start, tk), :]
        bias = [_dot(dropped[g], e_ref[kt]) for g in range(NSA_GROUPS)]
        s = [_dot_nt(qm[g][i], k) + bias[g] for g, i in heads]
        if masked:
            ok = (kt * tk + key_in_tile) <= t_col
            s = [jnp.where(ok, x, NEG_INF) for x in s]
        return _online_steps(s, [v] * n_heads, carry)

    n_kt = (q0 + tq + tk - 1) // tk
    carry = lax.fori_loop(0, n_kt - 1, sel_tile, _online_init(tq, n_heads))
    o_sel = [acc / l for _, l, acc in sel_tile(n_kt - 1, carry, True)]

    span = WINDOW + tq
    start = pl.multiple_of(jnp.maximum(q0 - WINDOW, 0), tq)
    key = start + lax.broadcasted_iota(jnp.int32, (1, span), 1)
    ok = (key <= t_col) & (key > t_col - WINDOW)
    k = kw_ref[pl.ds(start, span), :]
    v = vw_ref[pl.ds(start, span), :]
    o_win = []
    for g in range(NSA_GROUPS):
        s_win = [jnp.where(ok, _dot_nt(qm[g][i], k), NEG_INF) for i in range(NSA_HPG)]
        p_win = [jnp.exp(s - jnp.max(s, axis=-1, keepdims=True)) for s in s_win]
        pv_win = [_dot(p.astype(BF16), v) for p in p_win]
        o_win += [pv / jnp.sum(p, axis=-1, keepdims=True) for p, pv in zip(p_win, pv_win)]

    gates = jax.nn.sigmoid(gl_ref[...])
    for i in range(NSA_HPG):
        outs = []
        for g in range(NSA_GROUPS):
            hd = g * NSA_HPG + i
            outs.append(gates[:, 3 * hd:3 * hd + 1] * o_cmp[hd]
                        + gates[:, 3 * hd + 1:3 * hd + 2] * o_sel[hd]
                        + gates[:, 3 * hd + 2:3 * hd + 3] * o_win[hd])
        o_ref[:, i * LANES:(i + 1) * LANES] = jnp.where(half[0], outs[0], outs[1]).astype(o_ref.dtype)


def _nsa(qn, kcmp, vcmp, ks, vs, kw, vw, gl, ct, e_mat, batch, seq, tq, tk):
    nq = seq // tq
    n_cmp_rows = seq // CMP_STRIDE
    n_slc = seq // SEL_BLOCK
    row_blk = lambda w: pl.BlockSpec((tq, w), lambda b, q: (b * nq + q, 0))
    seq_blk = pl.BlockSpec((seq, LANES), lambda b, q: (b, 0))
    cmp_blk = pl.BlockSpec((None, n_cmp_rows, LANES), lambda b, q: (b, 0, 0))
    return pl.pallas_call(
        functools.partial(_nsa_kernel, tq=tq, tk=tk, n_cmp_rows=n_cmp_rows, n_slc=n_slc),
        grid=(batch, nq),
        in_specs=[row_blk(NSA_Q_W), cmp_blk, cmp_blk, seq_blk, seq_blk, seq_blk, seq_blk, row_blk(LANES),
                  _resident(ct.shape), _resident(e_mat.shape)],
        out_specs=row_blk(NSA_Q_W),
        out_shape=jax.ShapeDtypeStruct((batch * seq, NSA_Q_W), BF16),
        compiler_params=_params(("parallel", "arbitrary")),
        name="nsa_attention",
    )(qn, kcmp, vcmp, ks, vs, kw, vw, gl, ct, e_mat)


def _diff_kernel(q_ref, k_ref, v_ref, lam_ref, sub_ref, o_ref, *, tq, tk, n_heads, lam_init):
    qi = pl.program_id(2)
    n_diag = max(tq // tk, 1)
    lane = lax.broadcasted_iota(jnp.int32, (1, LANES), 1)
    row = lax.broadcasted_iota(jnp.int32, (tq, 1), 0)
    key_in_tile = lax.broadcasted_iota(jnp.int32, (1, tk), 1)
    zero = jnp.zeros((), BF16)
    qs = []
    for h in range(n_heads):
        q = q_ref[:, h * LANES:(h + 1) * LANES]
        qs += [jnp.where(lane < HEAD_DIM, q, zero), jnp.where(lane >= HEAD_DIM, q, zero)]

    def tile(kt, carry, masked=False):
        start = pl.multiple_of(kt * tk, tk)
        k = [k_ref[pl.ds(start, tk), h * LANES:(h + 1) * LANES] for h in range(n_heads)]
        v = [v_ref[pl.ds(start, tk), h * LANES:(h + 1) * LANES] for h in range(n_heads)]
        s = [_dot_nt(qs[c], k[c // 2]) for c in range(2 * n_heads)]
        if masked:
            ok = (kt * tk + key_in_tile) <= (qi * tq + row)
            s = [jnp.where(ok, x, NEG_INF) for x in s]
        return _online_steps(s, [v[c // 2] for c in range(2 * n_heads)], carry)

    n_full = (qi * tq) // tk
    carry = lax.fori_loop(0, n_full, tile, _online_init(tq, 2 * n_heads))
    for j in range(n_diag):
        carry = tile(n_full + j, carry, True)

    lam_rows = lam_ref[...]
    lam = (jnp.exp(jnp.sum(lam_rows[0:1] * lam_rows[1:2], axis=-1, keepdims=True))
           - jnp.exp(jnp.sum(lam_rows[2:3] * lam_rows[3:4], axis=-1, keepdims=True)) + lam_init)
    for h in range(n_heads):
        (_, l1, a1), (_, l2, a2) = carry[2 * h], carry[2 * h + 1]
        o = a1 / l1 - lam * (a2 / l2)
        o_ref[:, h * LANES:(h + 1) * LANES] = (_rms(o, sub_ref[...]) * (1.0 - lam_init)).astype(o_ref.dtype)


def _diff(dq, dk, dv, lam_rows, subln, lam_init, batch, seq, tq, tk, n_heads=2):
    nq = seq // tq
    w = n_heads * LANES
    return pl.pallas_call(
        functools.partial(_diff_kernel, tq=tq, tk=tk, n_heads=n_heads, lam_init=lam_init),
        grid=(batch, DIFF_HEADS // n_heads, nq),
        in_specs=[pl.BlockSpec((tq, w), lambda b, h, q: (b * nq + q, h)),
                  pl.BlockSpec((seq, w), lambda b, h, q: (b, h)),
                  pl.BlockSpec((seq, w), lambda b, h, q: (b, h)),
                  _resident(lam_rows.shape), _resident(subln.shape)],
        out_specs=pl.BlockSpec((tq, w), lambda b, h, q: (b * nq + q, h)),
        out_shape=jax.ShapeDtypeStruct((batch * seq, DIFF_V_W), BF16),
        compiler_params=_params(("parallel", "parallel", "arbitrary")),
        name="diff_attention",
    )(dq, dk, dv, lam_rows, subln)


def _sb_kernel(q_ref, k_ref, v_ref, u_ref, o_ref, *, tq, tk, n_pairs):
    qi = pl.program_id(2)
    n_diag = tq // tk
    n_ch = 2 * n_pairs
    lane = lax.broadcasted_iota(jnp.int32, (1, LANES), 1)
    row = lax.broadcasted_iota(jnp.int32, (tq, 1), 0)
    key_in_tile = lax.broadcasted_iota(jnp.int32, (1, tk), 1)
    zero = jnp.zeros((), BF16)
    qa = []
    for p in range(n_pairs):
        q = q_ref[:, p * LANES:(p + 1) * LANES]
        qa += [jnp.where(lane < HEAD_DIM, q, zero), jnp.where(lane >= HEAD_DIM, q, zero)]
    u2 = u_ref[...]

    def rows(ref, kt, p):
        return ref[pl.ds(pl.multiple_of(kt * tk, tk), tk), p * LANES:(p + 1) * LANES]

    def sweep(kts, carry, stricts=None):
        c, acc = carry
        chains = [(i, h) for i in range(len(kts)) for h in range(n_ch)]
        strict = {i: None if stricts is None else stricts[i] for i in range(len(kts))}
        z = {}
        for i, kt in enumerate(kts):
            k = [rows(k_ref, kt, p) for p in range(n_pairs)]
            for h in range(n_ch):
                z[i, h] = _dot_nt(qa[h], k[h // 2])
        nlr, pieces = {}, {}
        for ch in chains:
            soft = jnp.log(1.0 + jnp.exp(-jnp.abs(z[ch])))
            n = jnp.maximum(z[ch], 0.0) + soft
            if strict[ch[0]] is not None:
                n = jnp.where(strict[ch[0]], n, 0.0)
            hi = n.astype(BF16)
            lo = (n - hi.astype(F32)).astype(BF16)
            nlr[ch] = n
            pieces[ch] = jnp.concatenate([hi, lo], axis=1)
        nsuf = {ch: _dot(pieces[ch], u2) for ch in chains}
        w = {}
        offset = list(c)
        for i in range(len(kts)):
            for h in range(n_ch):
                wa = jnp.exp(z[i, h] - nlr[i, h] - nsuf[i, h] - offset[h])
                if strict[i] is not None:
                    wa = jnp.where(strict[i], wa, 0.0)
                w[i, h] = wa.astype(BF16)
                offset[h] = offset[h] + nsuf[i, h][:, 0:1] + nlr[i, h][:, 0:1]
        v = {(i, p): rows(v_ref, kt, p) for i, kt in enumerate(kts) for p in range(n_pairs)}
        pv = {ch: _dot(w[ch], v[ch[0], ch[1] // 2]) for ch in chains}
        acc = tuple(acc[h] + sum(pv[i, h] for i in range(len(kts))) for h in range(n_ch))
        return tuple(offset), acc

    def smallest(c):
        return functools.reduce(jnp.minimum, [jnp.min(x) for x in c])

    carry = (tuple(jnp.zeros((tq, 1), F32) for _ in range(n_ch)),
             tuple(jnp.zeros((tq, LANES), F32) for _ in range(n_ch)))
    diag = list(reversed(range(n_diag)))
    carry = sweep([qi * n_diag + j for j in diag], carry, [(j * tk + key_in_tile) < row for j in diag])
    n_off = qi * n_diag

    def more(state):
        r, c_min, _ = state
        return jnp.logical_and(r < n_off, c_min < SB_UNDERFLOW)

    def step(state):
        r, _, cr = state
        cr = sweep([n_off - 1 - r], cr)
        return r + 1, smallest(cr[0]), cr

    _, _, (_, acc) = lax.while_loop(more, step, (jnp.int32(0), smallest(carry[0]), carry))
    for p in range(n_pairs):
        o_ref[:, p * LANES:(p + 1) * LANES] = jnp.where(
            lane < HEAD_DIM, acc[2 * p], acc[2 * p + 1]).astype(o_ref.dtype)


def _sb(qkv, u_mat, batch, seq, tq, tk, n_pairs):
    nq = seq // tq
    n_blocks = SB_HEADS // 2 // n_pairs
    w = n_pairs * LANES
    return pl.pallas_call(
        functools.partial(_sb_kernel, tq=tq, tk=tk, n_pairs=n_pairs),
        grid=(batch, n_blocks, nq),
        in_specs=[pl.BlockSpec((tq, w), lambda b, p, q: (b * nq + q, p)),
                  pl.BlockSpec((seq, w), lambda b, p, q: (b, n_blocks + p)),
                  pl.BlockSpec((seq, w), lambda b, p, q: (b, 2 * n_blocks + p)),
                  _resident(u_mat.shape)],
        out_specs=pl.BlockSpec((tq, w), lambda b, p, q: (b * nq + q, p)),
        out_shape=jax.ShapeDtypeStruct((batch * seq, D_MODEL), BF16),
        compiler_params=_params(("parallel", "parallel", "arbitrary")),
        name="sb_attention",
    )(qkv, qkv, qkv, u_mat)


def _mix_ffn_kernel(x_ref, oa_ref, ob_ref, wo_ref, g_ref, wg_ref, wu_ref, wd_ref, gf_ref, out_ref, *,
                    final_norm):
    half = oa_ref.shape[1]
    x = x_ref[...] + _dot(oa_ref[...], wo_ref[0:half, :]) + _dot(ob_ref[...], wo_ref[half:2 * half, :])
    h = _rms(x, g_ref[...]).astype(BF16)
    gate = _dot(h, wg_ref[...])
    up = _dot(h, wu_ref[...])
    act = (gate * jax.nn.sigmoid(gate) * up).astype(BF16)
    acc = x + _dot(act, wd_ref[...])
    if final_norm:
        acc = _rms(acc, gf_ref[...])
    out_ref[...] = acc


def _mix_ffn(x2, oa, ob, oa_blk, ob_blk, wo, g, wg, wu, wd, gf, final_norm, tm=512):
    T = x2.shape[0]
    half = D_MODEL // 2
    return pl.pallas_call(
        functools.partial(_mix_ffn_kernel, final_norm=final_norm),
        grid=(T // tm,),
        in_specs=[pl.BlockSpec((tm, D_MODEL), lambda m: (m, 0)),
                  pl.BlockSpec((tm, half), lambda m: (m, oa_blk)),
                  pl.BlockSpec((tm, half), lambda m: (m, ob_blk)),
                  _resident(wo.shape), _resident((1, D_MODEL)),
                  _resident(wg.shape), _resident(wu.shape), _resident(wd.shape), _resident((1, D_MODEL))],
        out_specs=pl.BlockSpec((tm, D_MODEL), lambda m: (m, 0)),
        out_shape=jax.ShapeDtypeStruct((T, D_MODEL), F32),
        compiler_params=_params(("parallel",)),
        name="outproj_ffn",
    )(x2, oa, ob, wo, g.reshape(1, D_MODEL), wg, wu, wd, gf.reshape(1, D_MODEL))


def _even_in_columns():
    off = np.cumsum([0, NSA_Q_W, NSA_KV_W, NSA_KV_W, NSA_KV_W, NSA_KV_W, NSA_KV_W, NSA_KV_W,
                     NSA_GATE_W, DIFF_QK_W, DIFF_QK_W, DIFF_V_W])
    nq, kc, vc, ks, vs, kw, vw, gl, dq, dk, dv = off[:-1]
    rng = lambda s, w: np.arange(s, s + w)
    q_cols = np.concatenate([np.concatenate([rng(nq + HEAD_DIM * i, HEAD_DIM),
                                             rng(nq + HEAD_DIM * (NSA_HPG + i), HEAD_DIM)])
                             for i in range(NSA_HPG)])
    pieces = [(q_cols, True), (rng(kc, NSA_KV_W), True), (rng(ks, NSA_KV_W), True), (rng(kw, NSA_KV_W), True),
              (rng(dq, DIFF_QK_W), True), (rng(dk, DIFF_QK_W), True),
              (rng(vc, NSA_KV_W), False), (rng(vs, NSA_KV_W), False), (rng(vw, NSA_KV_W), False),
              (rng(dv, DIFF_V_W), False), (rng(gl, NSA_GATE_W), False)]
    return pieces


def _prep_even_w_in(w):
    pieces = _even_in_columns()
    scale = HEAD_DIM ** -0.5
    cols, widths = [], []
    for idx, (src, _) in enumerate(pieces):
        blk = w[:, src]
        if idx in (0, 4):
            blk = blk * scale
        pad = (-blk.shape[1]) % LANES
        if pad:
            blk = jnp.pad(blk, ((0, 0), (0, pad)))
        cols.append(blk)
        widths.append(blk.shape[1])
    n_rope = sum(1 for _, r in pieces if r)
    return jnp.concatenate(cols, axis=1).astype(BF16), widths, n_rope


def _nsa_out_rows():
    rows = []
    for i in range(NSA_HPG):
        for g in range(NSA_GROUPS):
            hd = g * NSA_HPG + i
            rows.append(np.arange(hd * HEAD_DIM, (hd + 1) * HEAD_DIM))
    return np.concatenate(rows + [np.arange(NSA_Q_W, NSA_Q_W + DIFF_V_W)])


def _prep_cmp(pos, w):
    half = CMP_BLOCK // 2
    p2 = jnp.tile(pos.reshape(2, half, 1, HEAD_DIM), (1, 1, NSA_GROUPS, 1)).reshape(2, half * LANES)
    w4 = w.reshape(2, half, HEAD_DIM, HEAD_DIM).astype(BF16)
    zeros = jnp.zeros_like(w4)
    wbd = jnp.stack([jnp.concatenate([w4 if c == g else zeros for c in range(NSA_GROUPS)], axis=-1)
                     for g in range(NSA_GROUPS)], axis=2)
    return p2.astype(F32), wbd.reshape(2, half * LANES, LANES)


def _rope_tables(seq):
    inv = 1.0 / (ROPE_THETA ** (jnp.arange(0, HEAD_DIM, 2, dtype=F32) / HEAD_DIM))
    ang = jnp.arange(seq, dtype=F32)[:, None] * inv[None, :]
    cos, sin = jnp.cos(ang), jnp.sin(ang)
    reps = LANES // HEAD_DIM
    cos_t = jnp.tile(jnp.concatenate([cos, cos], axis=1), (1, reps))
    sin_t = jnp.tile(jnp.concatenate([-sin, sin], axis=1), (1, reps))
    return cos_t, sin_t


def _selection_constants(seq, sb_tk, nsa_tk):
    n_cmp_rows = seq // CMP_STRIDE
    n_slc = seq // SEL_BLOCK
    n = np.arange(n_cmp_rows)[None, :] * CMP_STRIDE
    j = np.arange(n_slc)[:, None] * SEL_BLOCK
    ct = ((n < j + SEL_BLOCK) & (n + CMP_BLOCK > j)).astype(np.float32)
    e = np.zeros((LANES, seq), np.float32)
    e[np.arange(seq) // SEL_BLOCK, np.arange(seq)] = NEG_INF
    e = e.reshape(LANES, seq // nsa_tk, nsa_tk).transpose(1, 0, 2)
    u =(np.arange(sb_tk)[:, None] > np.arange(sb_tk)[None, :]).astype(np.float32)
    return jnp.asarray(ct, BF16), jnp.asarray(e, BF16), jnp.asarray(np.concatenate([u, u], axis=0), BF16)


def kernel(x, norm_mix, norm_ffn, norm_final, even_w_in, even_w_out, cmp_pos_k, cmp_w_k, cmp_pos_v, cmp_w_v,
           diff_lq1, diff_lk1, diff_lq2, diff_lk2, diff_subln, odd_w_in, odd_w_out, ffn_w_gate, ffn_w_up,
           ffn_w_down):
    batch, seq, _ = x.shape
    nsa_tq, nsa_tk = 256, 512
    sb_tq, sb_tk, sb_pairs = 256, 256, 2
    diff_tq, diff_tk = 256, 1024
    assert seq % sb_tq == 0 and seq % diff_tk == 0 and seq // SEL_BLOCK <= LANES and seq // SEL_BLOCK >= SEL_TOPK
    cos_t, sin_t = _rope_tables(seq)
    ct, e_mat, u_mat = _selection_constants(seq, sb_tk, nsa_tk)
    out_rows = _nsa_out_rows()
    sb_scale = jnp.concatenate([jnp.full((D_MODEL,), HEAD_DIM ** -0.5, F32), jnp.ones((2 * D_MODEL,), F32)])

    x2 = x.reshape(batch * seq, D_MODEL)
    for layer in range(DEPTH):
        last = layer == DEPTH - 1
        if layer % 2 == 0:
            e = layer // 2
            lam_init = 0.8 - 0.6 * math.exp(-0.3 * layer)
            w_in, widths, n_rope = _prep_even_w_in(even_w_in[e])
            qn, kc, ks, kw, dq, dk, vc, vs, vw, dv, gl = _inproj(
                x2, norm_mix[layer], w_in, cos_t, sin_t, widths, n_rope, True, seq)
            pk, wk = _prep_cmp(cmp_pos_k[e], cmp_w_k[e])
            pv, wv = _prep_cmp(cmp_pos_v[e], cmp_w_v[e])
            kcmp, vcmp = _compress(kc, vc, pk, pv, wk, wv, batch, seq)
            o_nsa = _nsa(qn, kcmp, vcmp, ks, vs, kw, vw, gl, ct, e_mat, batch, seq, nsa_tq, nsa_tk)
            lam_rows = jnp.zeros((8, LANES), F32).at[0:4, 0:HEAD_DIM].set(
                jnp.stack([diff_lq1[e], diff_lk1[e], diff_lq2[e], diff_lk2[e]]))
            o_diff = _diff(dq, dk, dv, lam_rows, diff_subln[e].reshape(1, DIFF_VDIM), lam_init, batch, seq,
                           diff_tq, diff_tk)
            wo = even_w_out[e][out_rows].astype(BF16)
            oa, ob, oa_blk, ob_blk = o_nsa, o_diff, 0, 0
        else:
            o = layer // 2
            w_in = (odd_w_in[o] * sb_scale[None, :]).astype(BF16)
            (qkv,) = _inproj(x2, norm_mix[layer], w_in, cos_t, sin_t, [3 * D_MODEL], 0, False, seq)
            o_sb = _sb(qkv, u_mat, batch, seq, sb_tq, sb_tk, sb_pairs)
            wo = odd_w_out[o].astype(BF16)
            oa, ob, oa_blk, ob_blk = o_sb, o_sb, 0, 1
        x2 = _mix_ffn(x2, oa, ob, oa_blk, ob_blk, wo, norm_ffn[layer],
                      ffn_w_gate[layer].astype(BF16), ffn_w_up[layer].astype(BF16),
                      ffn_w_down[layer].astype(BF16), norm_final, last)
    return x2.reshape(batch, seq, D_MODEL)
```

```python
import functools
import math

import numpy as np
import jax
import jax.numpy as jnp
from jax import lax
from jax.experimental import pallas as pl
from jax.experimental.pallas import tpu as pltpu

D_MODEL = 1024
DEPTH = 4
HEAD_DIM = 64
ROPE_THETA = 10000.0
NORM_EPS = 1e-6
NEG_INF = -1e30
BIG = 1e30
SB_UNDERFLOW = 110.0

NSA_HEADS = 8
NSA_GROUPS = 2
NSA_HPG = NSA_HEADS // NSA_GROUPS
CMP_BLOCK = 32
CMP_STRIDE = 16
SEL_BLOCK = 64
SEL_TOPK = 16
WINDOW = 512

DIFF_HEADS = 4
DIFF_VDIM = 2 * HEAD_DIM

SB_HEADS = D_MODEL // HEAD_DIM
FFN_HIDDEN = -(-(8 * D_MODEL) // (3 * 256)) * 256

NSA_Q_W = NSA_HEADS * HEAD_DIM
NSA_KV_W = NSA_GROUPS * HEAD_DIM
NSA_GATE_W = 3 * NSA_HEADS
DIFF_QK_W = DIFF_HEADS * 2 * HEAD_DIM
DIFF_V_W = DIFF_HEADS * DIFF_VDIM

LANES = 128
SUBLANES = 8
VMEM_LIMIT = 56 * 1024 * 1024

BF16 = jnp.bfloat16
F32 = jnp.float32


def _dot(a, b):
    return jnp.dot(a, b, preferred_element_type=F32)


def _dot_nt(a, b):
    return lax.dot_general(a, b, (((1,), (1,)), ((), ())), preferred_element_type=F32)


def _rms(x, g):
    return x * lax.rsqrt(jnp.mean(x * x, axis=-1, keepdims=True) + NORM_EPS) * g


def _params(sem, vmem=VMEM_LIMIT):
    return pltpu.CompilerParams(dimension_semantics=sem, vmem_limit_bytes=vmem)


def _resident(shape):
    nd = len(shape)
    return pl.BlockSpec(shape, lambda *_: (0,) * nd, pipeline_mode=pl.Buffered(1))


def _inproj_kernel(x_ref, g_ref, w_ref, cos_ref, sin_ref, *out_refs, widths, n_rope, gate_last):
    h = _rms(x_ref[...], g_ref[...]).astype(BF16)
    lane = lax.broadcasted_iota(jnp.int32, (1, LANES), 1)
    first_half = (lane % HEAD_DIM) < (HEAD_DIM // 2)
    rope_w = sum(widths[:n_rope])
    total_w = sum(widths)
    y_rope = _dot(h, w_ref[:, 0:rope_w]) if n_rope else None
    y_rest = _dot(h, w_ref[:, rope_w:total_w])
    col = 0
    for oi, (o_ref, w) in enumerate(zip(out_refs, widths)):
        if oi < n_rope:
            cos, sin = cos_ref[...], sin_ref[...]
            for b in range(w // LANES):
                yb = y_rope[:, col + b * LANES:col + (b + 1) * LANES]
                partner = jnp.where(first_half, pltpu.roll(yb, LANES - HEAD_DIM // 2, 1),
                                    pltpu.roll(yb, HEAD_DIM // 2, 1))
                o_ref[:, b * LANES:(b + 1) * LANES] = (yb * cos + partner * sin).astype(o_ref.dtype)
        else:
            o_ref[...] = y_rest[:, col - rope_w:col - rope_w + w].astype(o_ref.dtype)
        col += w


def _inproj(x2, g, w, cos_t, sin_t, widths, n_rope, gate_last, seq, tm=512):
    T = x2.shape[0]
    n_pos_blocks = seq // tm
    out_shape = [jax.ShapeDtypeStruct((T, wd), F32 if (gate_last and i == len(widths) - 1) else BF16)
                 for i, wd in enumerate(widths)]
    out_specs = [pl.BlockSpec((tm, wd), lambda m: (m, 0)) for wd in widths]
    return pl.pallas_call(
        functools.partial(_inproj_kernel, widths=tuple(widths), n_rope=n_rope, gate_last=gate_last),
        grid=(T // tm,),
        in_specs=[pl.BlockSpec((tm, D_MODEL), lambda m: (m, 0)),
                  _resident((1, D_MODEL)),
                  _resident(w.shape),
                  pl.BlockSpec((tm, LANES), lambda m: (m % n_pos_blocks, 0)),
                  pl.BlockSpec((tm, LANES), lambda m: (m % n_pos_blocks, 0))],
        out_specs=out_specs,
        out_shape=out_shape,
        compiler_params=_params(("parallel",)),
        name="inproj",
    )(x2, g.reshape(1, D_MODEL), w, cos_t, sin_t)


def _compress_kernel(kc_ref, vc_ref, pk_ref, pv_ref, wk_ref, wv_ref, ko_ref, vo_ref):
    def one(x_ref, p_ref, w_ref, o_ref):
        x = x_ref[...].astype(F32)
        first = _dot((x + p_ref[0:1, :]).astype(BF16), w_ref[0])
        second = _dot((x + p_ref[1:2, :]).astype(BF16), w_ref[1])
        n = second.shape[0]
        o_ref[...] = (first + pltpu.roll(second, n - 1, 0)).astype(o_ref.dtype)

    one(kc_ref, pk_ref, wk_ref, ko_ref)
    one(vc_ref, pv_ref, wv_ref, vo_ref)


def _compress(kc, vc, pk, pv, wk, wv, batch, seq):
    rows = seq // CMP_STRIDE
    width = CMP_STRIDE * LANES
    kc3 = kc.reshape(batch, rows, width)
    vc3 = vc.reshape(batch, rows, width)
    blk = pl.BlockSpec((None, rows, width), lambda b: (b, 0, 0))
    oblk = pl.BlockSpec((None, rows, LANES), lambda b: (b, 0, 0))
    return pl.pallas_call(
        _compress_kernel,
        grid=(batch,),
        in_specs=[blk, blk, _resident(pk.shape), _resident(pv.shape), _resident(wk.shape), _resident(wv.shape)],
        out_specs=[oblk, oblk],
        out_shape=[jax.ShapeDtypeStruct((batch, rows, LANES), BF16)] * 2,
        compiler_params=_params(("parallel",)),
        name="nsa_compress",
    )(kc3, vc3, pk, pv, wk, wv)


def _online_steps(scores, values, carries):
    m_new = [jnp.maximum(m, jnp.max(s, axis=-1, keepdims=True)) for s, (m, _, _) in zip(scores, carries)]
    p = [jnp.exp(s - mn) for s, mn in zip(scores, m_new)]
    pv = [_dot(pi.astype(BF16), v) for pi, v in zip(p, values)]
    out = []
    for pi, pvi, mn, (m, l, acc) in zip(p, pv, m_new, carries):
        alpha = jnp.exp(m - mn)
        out.append((mn, alpha * l + jnp.sum(pi, axis=-1, keepdims=True), alpha * acc + pvi))
    return out


def _online_init(tq, n):
    return [(jnp.full((tq, 1), NEG_INF, F32), jnp.zeros((tq, 1), F32), jnp.zeros((tq, LANES), F32))
            for _ in range(n)]


def _nsa_kernel(q_ref, kcmp_ref, vcmp_ref, ks_ref, vs_ref, kw_ref, vw_ref, gl_ref, ct_ref, e_ref,
                o_ref, *, tq, tk, n_cmp_rows, n_slc):
    qi = pl.program_id(1)
    q0 = qi * tq
    lane = lax.broadcasted_iota(jnp.int32, (1, LANES), 1)
    half = [lane < HEAD_DIM, lane >= HEAD_DIM]
    t_col = q0 + lax.broadcasted_iota(jnp.int32, (tq, 1), 0)

    qm = [[jnp.where(half[g], q_ref[:, i * LANES:(i + 1) * LANES], jnp.zeros((), BF16))
           for i in range(NSA_HPG)] for g in range(NSA_GROUPS)]

    heads = [(g, i) for g in range(NSA_GROUPS) for i in range(NSA_HPG)]
    n_heads = len(heads)

    n_idx = lax.broadcasted_iota(jnp.int32, (1, n_cmp_rows), 1)
    cmp_ok = (n_idx * CMP_STRIDE + (CMP_BLOCK - 1)) <= t_col
    kcmp = kcmp_ref[...]
    vcmp = vcmp_ref[...]
    s_cmp = [_dot_nt(qm[g][i], kcmp) for g, i in heads]
    p_cmp = []
    for s in s_cmp:
        s = jnp.where(cmp_ok, s, NEG_INF)
        e = jnp.where(cmp_ok, jnp.exp(s - jnp.max(s, axis=-1, keepdims=True)), 0.0)
        l = jnp.sum(e, axis=-1, keepdims=True)
        p_cmp.append(e * jnp.where(l > 0.0, 1.0 / l, 0.0))
    o_cmp = [_dot(p.astype(BF16), vcmp) for p in p_cmp]
    p_sum = [sum(p_cmp[g * NSA_HPG + i] for i in range(NSA_HPG)) for g in range(NSA_GROUPS)]

    n_grp = n_slc // SUBLANES
    jb = lax.broadcasted_iota(jnp.int32, (n_slc, 1), 0)
    sub = lax.broadcasted_iota(jnp.int32, (SUBLANES, 1), 0)
    qblk = (q0 + lax.broadcasted_iota(jnp.int32, (1, tq), 1)) // SEL_BLOCK
    forced = (jb == 0) | (jb == qblk) | (jb == qblk - 1)
    future = jb > qblk
    key_in_tile = lax.broadcasted_iota(jnp.int32, (1, tk), 1)
    ct = ct_ref[...]
    dropped = []
    for g in range(NSA_GROUPS):
        r0 = p_sum[g]
        p_hi = r0.astype(BF16)
        r1 = r0 - p_hi.astype(F32)
        p_mid = r1.astype(BF16)
        p_lo = (r1 - p_mid.astype(F32)).astype(BF16)
        imp = _dot_nt(ct, p_hi) + _dot_nt(ct, p_mid) + _dot_nt(ct, p_lo)
        imp = jnp.where(forced, BIG, jnp.where(future, -BIG, imp))
        imp_g = [imp[r * SUBLANES:(r + 1) * SUBLANES, :] for r in range(n_grp)]
        cnt = [jnp.zeros((SUBLANES, tq), F32) for _ in range(n_grp)]
        for j in range(n_slc):
            r, jj = divmod(j, SUBLANES)
            row = imp_g[r][jj:jj + 1, :]
            for b in range(n_grp):
                if b < r:
                    inc = jnp.where(row > imp_g[b], 1.0, 0.0)
                elif b > r:
                    inc = jnp.where(row >= imp_g[b], 1.0, 0.0)
                else:
                    tie = jnp.where(sub > jj, 1.0, 0.0)
                    inc = jnp.where(row > imp_g[b], 1.0, jnp.where(row == imp_g[b], tie, 0.0))
                cnt[b] = cnt[b] + inc
        drop_t = [jnp.where((cnt[b] < float(SEL_TOPK)) & (imp_g[b] >= 0.0), 0.0, 1.0) for b in range(n_grp)]
        drop_t = jnp.concatenate(drop_t + [jnp.ones((LANES - n_slc, tq), F32)], axis=0)
        dropped.append(drop_t.T.astype(BF16))

    q_sel = [jnp.concatenate([qm[g][i], dropped[g]], axis=1) for g, i in heads]

    def sel_tile(kt, carry, masked=False):
        start = pl.multiple_of(kt * tk, tk)
        k = jnp.concatenate([ks_ref[pl.ds(start, tk), :], e_ref[pl.ds(start, tk), :]], axis=1)
        v = vs_ref[pl.ds(start, tk), :]
        s = [_dot_nt(q, k) for q in q_sel]
        if masked:
            ok = (kt * tk + key_in_tile) <= t_col
            s = [jnp.where(ok, x, NEG_INF) for x in s]
        return _online_steps(s, [v] * n_heads, carry)

    n_kt = (q0 + tq + tk - 1) // tk
    carry = lax.fori_loop(0, n_kt - 1, sel_tile, _online_init(tq, n_heads))
    o_sel = [acc / l for _, l, acc in sel_tile(n_kt - 1, carry, True)]

    span = WINDOW + tq
    start = pl.multiple_of(jnp.maximum(q0 - WINDOW, 0), tq)
    key = start + lax.broadcasted_iota(jnp.int32, (1, span), 1)
    ok = (key <= t_col) & (key > t_col - WINDOW)
    k = kw_ref[pl.ds(start, span), :]
    v = vw_ref[pl.ds(start, span), :]
    o_win = []
    for g in range(NSA_GROUPS):
        s_win = [jnp.where(ok, _dot_nt(qm[g][i], k), NEG_INF) for i in range(NSA_HPG)]
        p_win = [jnp.exp(s - jnp.max(s, axis=-1, keepdims=True)) for s in s_win]
        pv_win = [_dot(p.astype(BF16), v) for p in p_win]
        o_win += [pv / jnp.sum(p, axis=-1, keepdims=True) for p, pv in zip(p_win, pv_win)]

    gates = jax.nn.sigmoid(gl_ref[...])
    for i in range(NSA_HPG):
        outs = []
        for g in range(NSA_GROUPS):
            hd = g * NSA_HPG + i
            outs.append(gates[:, 3 * hd:3 * hd + 1] * o_cmp[hd]
                        + gates[:, 3 * hd + 1:3 * hd + 2] * o_sel[hd]
                        + gates[:, 3 * hd + 2:3 * hd + 3] * o_win[hd])
        o_ref[:, i * LANES:(i + 1) * LANES] = jnp.where(half[0], outs[0], outs[1]).astype(o_ref.dtype)


def _nsa(qn, kcmp, vcmp, ks, vs, kw, vw, gl, ct, e_mat, batch, seq, tq, tk):
    nq = seq // tq
    n_cmp_rows = seq // CMP_STRIDE
    n_slc = seq // SEL_BLOCK
    row_blk = lambda w: pl.BlockSpec((tq, w), lambda b, q: (b * nq + q, 0))
    seq_blk = pl.BlockSpec((seq, LANES), lambda b, q: (b, 0))
    cmp_blk = pl.BlockSpec((None, n_cmp_rows, LANES), lambda b, q: (b, 0, 0))
    return pl.pallas_call(
        functools.partial(_nsa_kernel, tq=tq, tk=tk, n_cmp_rows=n_cmp_rows, n_slc=n_slc),
        grid=(batch, nq),
        in_specs=[row_blk(NSA_Q_W), cmp_blk, cmp_blk, seq_blk, seq_blk, seq_blk, seq_blk, row_blk(LANES),
                  _resident(ct.shape), _resident(e_mat.shape)],
        out_specs=row_blk(NSA_Q_W),
        out_shape=jax.ShapeDtypeStruct((batch * seq, NSA_Q_W), BF16),
        compiler_params=_params(("parallel", "arbitrary")),
        name="nsa_attention",
    )(qn, kcmp, vcmp, ks, vs, kw, vw, gl, ct, e_mat)


def _diff_kernel(q_ref, k_ref, v_ref, lam_ref, sub_ref, o_ref, *, tq, tk, n_heads, lam_init):
    qi = pl.program_id(2)
    n_diag = max(tq // tk, 1)
    lane = lax.broadcasted_iota(jnp.int32, (1, LANES), 1)
    row = lax.broadcasted_iota(jnp.int32, (tq, 1), 0)
    key_in_tile = lax.broadcasted_iota(jnp.int32, (1, tk), 1)
    zero = jnp.zeros((), BF16)
    qs = []
    for h in range(n_heads):
        q = q_ref[:, h * LANES:(h + 1) * LANES]
        qs += [jnp.where(lane < HEAD_DIM, q, zero), jnp.where(lane >= HEAD_DIM, q, zero)]

    def tile(kt, carry, masked=False):
        start = pl.multiple_of(kt * tk, tk)
        k = [k_ref[pl.ds(start, tk), h * LANES:(h + 1) * LANES] for h in range(n_heads)]
        v = [v_ref[pl.ds(start, tk), h * LANES:(h + 1) * LANES] for h in range(n_heads)]
        s = [_dot_nt(qs[c], k[c // 2]) for c in range(2 * n_heads)]
        if masked:
            ok = (kt * tk + key_in_tile) <= (qi * tq + row)
            s = [jnp.where(ok, x, NEG_INF) for x in s]
        return _online_steps(s, [v[c // 2] for c in range(2 * n_heads)], carry)

    n_full = (qi * tq) // tk
    carry = lax.fori_loop(0, n_full, tile, _online_init(tq, 2 * n_heads))
    for j in range(n_diag):
        carry = tile(n_full + j, carry, True)

    lam_rows = lam_ref[...]
    lam = (jnp.exp(jnp.sum(lam_rows[0:1] * lam_rows[1:2], axis=-1, keepdims=True))
           - jnp.exp(jnp.sum(lam_rows[2:3] * lam_rows[3:4], axis=-1, keepdims=True)) + lam_init)
    for h in range(n_heads):
        (_, l1, a1), (_, l2, a2) = carry[2 * h], carry[2 * h + 1]
        o = a1 / l1 - lam * (a2 / l2)
        o_ref[:, h * LANES:(h + 1) * LANES] = (_rms(o, sub_ref[...]) * (1.0 - lam_init)).astype(o_ref.dtype)


def _diff(dq, dk, dv, lam_rows, subln, lam_init, batch, seq, tq, tk, n_heads=4):
    nq = seq // tq
    w = n_heads * LANES
    return pl.pallas_call(
        functools.partial(_diff_kernel, tq=tq, tk=tk, n_heads=n_heads, lam_init=lam_init),
        grid=(batch, DIFF_HEADS // n_heads, nq),
        in_specs=[pl.BlockSpec((tq, w), lambda b, h, q: (b * nq + q, h)),
                  pl.BlockSpec((seq, w), lambda b, h, q: (b, h)),
                  pl.BlockSpec((seq, w), lambda b, h, q: (b, h)),
                  _resident(lam_rows.shape), _resident(subln.shape)],
        out_specs=pl.BlockSpec((tq, w), lambda b, h, q: (b * nq + q, h)),
        out_shape=jax.ShapeDtypeStruct((batch * seq, DIFF_V_W), BF16),
        compiler_params=_params(("parallel", "parallel", "arbitrary")),
        name="diff_attention",
    )(dq, dk, dv, lam_rows, subln)


def _sb_kernel(q_ref, k_ref, v_ref, u_ref, o_ref, *, tq, tk, n_pairs):
    qi = pl.program_id(2)
    n_diag = tq // tk
    n_ch = 2 * n_pairs
    lane = lax.broadcasted_iota(jnp.int32, (1, LANES), 1)
    row = lax.broadcasted_iota(jnp.int32, (tq, 1), 0)
    key_in_tile = lax.broadcasted_iota(jnp.int32, (1, tk), 1)
    zero = jnp.zeros((), BF16)
    qa = []
    for p in range(n_pairs):
        q = q_ref[:, p * LANES:(p + 1) * LANES]
        qa += [jnp.where(lane < HEAD_DIM, q, zero), jnp.where(lane >= HEAD_DIM, q, zero)]
    u2 = u_ref[...]

    def rows(ref, kt, p):
        return ref[pl.ds(pl.multiple_of(kt * tk, tk), tk), p * LANES:(p + 1) * LANES]

    def sweep(kts, carry, stricts=None):
        c, acc = carry
        chains = [(i, h) for i in range(len(kts)) for h in range(n_ch)]
        strict = {i: None if stricts is None else stricts[i] for i in range(len(kts))}
        z = {}
        for i, kt in enumerate(kts):
            k = [rows(k_ref, kt, p) for p in range(n_pairs)]
            for h in range(n_ch):
                z[i, h] = _dot_nt(qa[h], k[h // 2])
        nlr, pieces = {}, {}
        for ch in chains:
            soft = jnp.log(1.0 + jnp.exp(-jnp.abs(z[ch])))
            n = jnp.maximum(z[ch], 0.0) + soft
            if strict[ch[0]] is not None:
                n = jnp.where(strict[ch[0]], n, 0.0)
            hi = n.astype(BF16)
            lo = (n - hi.astype(F32)).astype(BF16)
            nlr[ch] = n
            pieces[ch] = jnp.concatenate([hi, lo], axis=1)
        nsuf = {ch: _dot(pieces[ch], u2) for ch in chains}
        w = {}
        offset = list(c)
        for i in range(len(kts)):
            for h in range(n_ch):
                wa = jnp.exp(z[i, h] - nlr[i, h] - nsuf[i, h] - offset[h])
                if strict[i] is not None:
                    wa = jnp.where(strict[i], wa, 0.0)
                w[i, h] = wa.astype(BF16)
                offset[h] = offset[h] + nsuf[i, h][:, 0:1] + nlr[i, h][:, 0:1]
        v = {(i, p): rows(v_ref, kt, p) for i, kt in enumerate(kts) for p in range(n_pairs)}
        pv = {ch: _dot(w[ch], v[ch[0], ch[1] // 2]) for ch in chains}
        acc = tuple(acc[h] + sum(pv[i, h] for i in range(len(kts))) for h in range(n_ch))
        return tuple(offset), acc

    def smallest(c):
        return functools.reduce(jnp.minimum, [jnp.min(x) for x in c])

    carry = (tuple(jnp.zeros((tq, 1), F32) for _ in range(n_ch)),
             tuple(jnp.zeros((tq, LANES), F32) for _ in range(n_ch)))
    diag = list(reversed(range(n_diag)))
    carry = sweep([qi * n_diag + j for j in diag], carry, [(j * tk + key_in_tile) < row for j in diag])
    n_off = qi * n_diag

    def more(state):
        r, c_min, _ = state
        return jnp.logical_and(r < n_off, c_min < SB_UNDERFLOW)

    def step(state):
        r, _, cr = state
        cr = sweep([n_off - 1 - r], cr)
        return r + 1, smallest(cr[0]), cr

    _, _, (_, acc) = lax.while_loop(more, step, (jnp.int32(0), smallest(carry[0]), carry))
    for p in range(n_pairs):
        o_ref[:, p * LANES:(p + 1) * LANES] = jnp.where(
            lane < HEAD_DIM, acc[2 * p], acc[2 * p + 1]).astype(o_ref.dtype)


def _sb(qkv, u_mat, batch, seq, tq, tk, n_pairs):
    nq = seq // tq
    n_blocks = SB_HEADS // 2 // n_pairs
    w = n_pairs * LANES
    return pl.pallas_call(
        functools.partial(_sb_kernel, tq=tq, tk=tk, n_pairs=n_pairs),
        grid=(batch, n_blocks, nq),
        in_specs=[pl.BlockSpec((tq, w), lambda b, p, q: (b * nq + q, p)),
                  pl.BlockSpec((seq, w), lambda b, p, q: (b, n_blocks + p)),
                  pl.BlockSpec((seq, w), lambda b, p, q: (b, 2 * n_blocks + p)),
                  _resident(u_mat.shape)],
        out_specs=pl.BlockSpec((tq, w), lambda b, p, q: (b * nq + q, p)),
        out_shape=jax.ShapeDtypeStruct((batch * seq, D_MODEL), BF16),
        compiler_params=_params(("parallel", "parallel", "arbitrary")),
        name="sb_attention",
    )(qkv, qkv, qkv, u_mat)


def _mix_ffn_kernel(x_ref, oa_ref, ob_ref, wo_ref, g_ref, wg_ref, wu_ref, wd_ref, gf_ref, out_ref, *,
                    final_norm):
    half = oa_ref.shape[1]
    x = x_ref[...] + _dot(oa_ref[...], wo_ref[0:half, :]) + _dot(ob_ref[...], wo_ref[half:2 * half, :])
    h = _rms(x, g_ref[...]).astype(BF16)
    gate = _dot(h, wg_ref[...])
    up = _dot(h, wu_ref[...])
    act = (gate * jax.nn.sigmoid(gate) * up).astype(BF16)
    acc = x + _dot(act, wd_ref[...])
    if final_norm:
        acc = _rms(acc, gf_ref[...])
    out_ref[...] = acc


def _mix_ffn(x2, oa, ob, oa_blk, ob_blk, wo, g, wg, wu, wd, gf, final_norm, tm=512):
    T = x2.shape[0]
    half = D_MODEL // 2
    return pl.pallas_call(
        functools.partial(_mix_ffn_kernel, final_norm=final_norm),
        grid=(T // tm,),
        in_specs=[pl.BlockSpec((tm, D_MODEL), lambda m: (m, 0)),
                  pl.BlockSpec((tm, half), lambda m: (m, oa_blk)),
                  pl.BlockSpec((tm, half), lambda m: (m, ob_blk)),
                  _resident(wo.shape), _resident((1, D_MODEL)),
                  _resident(wg.shape), _resident(wu.shape), _resident(wd.shape), _resident((1, D_MODEL))],
        out_specs=pl.BlockSpec((tm, D_MODEL), lambda m: (m, 0)),
        out_shape=jax.ShapeDtypeStruct((T, D_MODEL), F32),
        compiler_params=_params(("parallel",)),
        name="outproj_ffn",
    )(x2, oa, ob, wo, g.reshape(1, D_MODEL), wg, wu, wd, gf.reshape(1, D_MODEL))


def _cast_kernel(w_ref, o_ref):
    o_ref[...] = w_ref[...].astype(o_ref.dtype)


def _to_bf16(w, layer, rows=256):
    _, n, c = w.shape
    return pl.pallas_call(
        _cast_kernel,
        grid=(n // rows,),
        in_specs=[pl.BlockSpec((None, rows, c), lambda i: (layer, i, 0))],
        out_specs=pl.BlockSpec((rows, c), lambda i: (i, 0)),
        out_shape=jax.ShapeDtypeStruct((n, c), BF16),
        compiler_params=_params(("parallel",)),
        name="cast_bf16",
    )(w)


def _even_in_columns():
    off = np.cumsum([0, NSA_Q_W, NSA_KV_W, NSA_KV_W, NSA_KV_W, NSA_KV_W, NSA_KV_W, NSA_KV_W,
                     NSA_GATE_W, DIFF_QK_W, DIFF_QK_W, DIFF_V_W])
    nq, kc, vc, ks, vs, kw, vw, gl, dq, dk, dv = off[:-1]
    rng = lambda s, w: np.arange(s, s + w)
    q_cols = np.concatenate([np.concatenate([rng(nq + HEAD_DIM * i, HEAD_DIM),
                                             rng(nq + HEAD_DIM * (NSA_HPG + i), HEAD_DIM)])
                             for i in range(NSA_HPG)])
    pieces = [(q_cols, True), (rng(kc, NSA_KV_W), True), (rng(ks, NSA_KV_W), True), (rng(kw, NSA_KV_W), True),
              (rng(dq, DIFF_QK_W), True), (rng(dk, DIFF_QK_W), True),
              (rng(vc, NSA_KV_W), False), (rng(vs, NSA_KV_W), False), (rng(vw, NSA_KV_W), False),
              (rng(dv, DIFF_V_W), False), (rng(gl, NSA_GATE_W), False)]
    return pieces


def _prep_even_w_in(w):
    pieces = _even_in_columns()
    scale = HEAD_DIM ** -0.5
    cols, widths = [], []
    for idx, (src, _) in enumerate(pieces):
        blk = w[:, src]
        if idx in (0, 4):
            blk = blk * scale
        pad = (-blk.shape[1]) % LANES
        if pad:
            blk = jnp.pad(blk, ((0, 0), (0, pad)))
        cols.append(blk)
        widths.append(blk.shape[1])
    n_rope = sum(1 for _, r in pieces if r)
    return jnp.concatenate(cols, axis=1).astype(BF16), widths, n_rope


def _nsa_out_rows():
    rows = []
    for i in range(NSA_HPG):
        for g in range(NSA_GROUPS):
            hd = g * NSA_HPG + i
            rows.append(np.arange(hd * HEAD_DIM, (hd + 1) * HEAD_DIM))
    return np.concatenate(rows + [np.arange(NSA_Q_W, NSA_Q_W + DIFF_V_W)])


def _prep_cmp(pos, w):
    half = CMP_BLOCK // 2
    p2 = jnp.tile(pos.reshape(2, half, 1, HEAD_DIM), (1, 1, NSA_GROUPS, 1)).reshape(2, half * LANES)
    w4 = w.reshape(2, half, HEAD_DIM, HEAD_DIM).astype(BF16)
    zeros = jnp.zeros_like(w4)
    wbd = jnp.stack([jnp.concatenate([w4 if c == g else zeros for c in range(NSA_GROUPS)], axis=-1)
                     for g in range(NSA_GROUPS)], axis=2)
    return p2.astype(F32), wbd.reshape(2, half * LANES, LANES)


def _rope_tables(seq):
    inv = 1.0 / (ROPE_THETA ** (jnp.arange(0, HEAD_DIM, 2, dtype=F32) / HEAD_DIM))
    ang = jnp.arange(seq, dtype=F32)[:, None] * inv[None, :]
    cos, sin = jnp.cos(ang), jnp.sin(ang)
    reps = LANES // HEAD_DIM
    cos_t = jnp.tile(jnp.concatenate([cos, cos], axis=1), (1, reps))
    sin_t = jnp.tile(jnp.concatenate([-sin, sin], axis=1), (1, reps))
    return cos_t, sin_t


def _selection_constants(seq, sb_tk):
    n_cmp_rows = seq // CMP_STRIDE
    n_slc = seq // SEL_BLOCK
    n = np.arange(n_cmp_rows)[None, :] * CMP_STRIDE
    j = np.arange(n_slc)[:, None] * SEL_BLOCK
    ct = ((n < j + SEL_BLOCK) & (n + CMP_BLOCK > j)).astype(np.float32)
    e = np.zeros((seq, LANES), np.float32)
    e[np.arange(seq), np.arange(seq) // SEL_BLOCK] = NEG_INF
    u =(np.arange(sb_tk)[:, None] > np.arange(sb_tk)[None, :]).astype(np.float32)
    return jnp.asarray(ct, BF16), jnp.asarray(e, BF16), jnp.asarray(np.concatenate([u, u], axis=0), BF16)


def kernel(x, norm_mix, norm_ffn, norm_final, even_w_in, even_w_out, cmp_pos_k, cmp_w_k, cmp_pos_v, cmp_w_v,
           diff_lq1, diff_lk1, diff_lq2, diff_lk2, diff_subln, odd_w_in, odd_w_out, ffn_w_gate, ffn_w_up,
           ffn_w_down):
    batch, seq, _ = x.shape
    nsa_tq, nsa_tk = 256, 512
    sb_tq, sb_tk, sb_pairs = 256, 256, 4
    diff_tq, diff_tk = 256, 1024
    assert seq % sb_tq == 0 and seq % diff_tk == 0 and seq // SEL_BLOCK <= LANES and seq // SEL_BLOCK >= SEL_TOPK
    cos_t, sin_t = _rope_tables(seq)
    ct, e_mat, u_mat = _selection_constants(seq, sb_tk)
    out_rows = _nsa_out_rows()
    sb_scale = jnp.concatenate([jnp.full((D_MODEL,), HEAD_DIM ** -0.5, F32), jnp.ones((2 * D_MODEL,), F32)])

    x2 = x.reshape(batch * seq, D_MODEL)
    for layer in range(DEPTH):
        last = layer == DEPTH - 1
        if layer % 2 == 0:
            e = layer // 2
            lam_init = 0.8 - 0.6 * math.exp(-0.3 * layer)
            w_in, widths, n_rope = _prep_even_w_in(even_w_in[e])
            qn, kc, ks, kw, dq, dk, vc, vs, vw, dv, gl = _inproj(
                x2, norm_mix[layer], w_in, cos_t, sin_t, widths, n_rope, True, seq)
            pk, wk = _prep_cmp(cmp_pos_k[e], cmp_w_k[e])
            pv, wv = _prep_cmp(cmp_pos_v[e], cmp_w_v[e])
            kcmp, vcmp = _compress(kc, vc, pk, pv, wk, wv, batch, seq)
            o_nsa = _nsa(qn, kcmp, vcmp, ks, vs, kw, vw, gl, ct, e_mat, batch, seq, nsa_tq, nsa_tk)
            lam_rows = jnp.zeros((8, LANES), F32).at[0:4, 0:HEAD_DIM].set(
                jnp.stack([diff_lq1[e], diff_lk1[e], diff_lq2[e], diff_lk2[e]]))
            o_diff = _diff(dq, dk, dv, lam_rows, diff_subln[e].reshape(1, DIFF_VDIM), lam_init, batch, seq,
                           diff_tq, diff_tk)
            wo = even_w_out[e][out_rows].astype(BF16)
            oa, ob, oa_blk, ob_blk = o_nsa, o_diff, 0, 0
        else:
            o = layer // 2
            w_in = (odd_w_in[o] * sb_scale[None, :]).astype(BF16)
            (qkv,) = _inproj(x2, norm_mix[layer], w_in, cos_t, sin_t, [3 * D_MODEL], 0, False, seq)
            o_sb = _sb(qkv, u_mat, batch, seq, sb_tq, sb_tk, sb_pairs)
            wo = odd_w_out[o].astype(BF16)
            oa, ob, oa_blk, ob_blk = o_sb, o_sb, 0, 1
        x2 = _mix_ffn(x2, oa, ob, oa_blk, ob_blk, wo, norm_ffn[layer],
                      _to_bf16(ffn_w_gate, layer), _to_bf16(ffn_w_up, layer),
                      _to_bf16(ffn_w_down, layer), norm_final, last)
    return x2.reshape(batch, seq, D_MODEL)
```

```python
import functools
import math

import numpy as np
import jax
import jax.numpy as jnp
from jax import lax
from jax.experimental import pallas as pl
from jax.experimental.pallas import tpu as pltpu

D_MODEL = 1024
DEPTH = 4
HEAD_DIM = 64
ROPE_THETA = 10000.0
NORM_EPS = 1e-6
NEG_INF = -1e30
BIG = 1e30
SOFTPLUS_LINEAR = 40.0
SB_UNDERFLOW = 110.0

NSA_HEADS = 8
NSA_GROUPS = 2
NSA_HPG = NSA_HEADS // NSA_GROUPS
CMP_BLOCK = 32
CMP_STRIDE = 16
SEL_BLOCK = 64
SEL_TOPK = 16
WINDOW = 512

DIFF_HEADS = 4
DIFF_VDIM = 2 * HEAD_DIM

SB_HEADS = D_MODEL // HEAD_DIM
FFN_HIDDEN = -(-(8 * D_MODEL) // (3 * 256)) * 256

NSA_Q_W = NSA_HEADS * HEAD_DIM
NSA_KV_W = NSA_GROUPS * HEAD_DIM
NSA_GATE_W = 3 * NSA_HEADS
DIFF_QK_W = DIFF_HEADS * 2 * HEAD_DIM
DIFF_V_W = DIFF_HEADS * DIFF_VDIM

LANES = 128
SUBLANES = 8
VMEM_LIMIT = 56 * 1024 * 1024

BF16 = jnp.bfloat16
F32 = jnp.float32


def _dot(a, b):
    return jnp.dot(a, b, preferred_element_type=F32)


def _dot_nt(a, b):
    return lax.dot_general(a, b, (((1,), (1,)), ((), ())), preferred_element_type=F32)


def _rms(x, g):
    return x * lax.rsqrt(jnp.mean(x * x, axis=-1, keepdims=True) + NORM_EPS) * g


def _params(sem, vmem=VMEM_LIMIT):
    return pltpu.CompilerParams(dimension_semantics=sem, vmem_limit_bytes=vmem)


def _resident(shape):
    nd = len(shape)
    return pl.BlockSpec(shape, lambda *_: (0,) * nd, pipeline_mode=pl.Buffered(1))


def _inproj_kernel(x_ref, g_ref, w_ref, cos_ref, sin_ref, *out_refs, widths, n_rope, gate_last):
    h = _rms(x_ref[...], g_ref[...]).astype(BF16)
    lane = lax.broadcasted_iota(jnp.int32, (1, LANES), 1)
    first_half = (lane % HEAD_DIM) < (HEAD_DIM // 2)
    rope_w = sum(widths[:n_rope])
    total_w = sum(widths)
    y_rope = _dot(h, w_ref[:, 0:rope_w]) if n_rope else None
    y_rest = _dot(h, w_ref[:, rope_w:total_w])
    col = 0
    for oi, (o_ref, w) in enumerate(zip(out_refs, widths)):
        if oi < n_rope:
            cos, sin = cos_ref[...], sin_ref[...]
            for b in range(w // LANES):
                yb = y_rope[:, col + b * LANES:col + (b + 1) * LANES]
                partner = jnp.where(first_half, pltpu.roll(yb, LANES - HEAD_DIM // 2, 1),
                                    pltpu.roll(yb, HEAD_DIM // 2, 1))
                o_ref[:, b * LANES:(b + 1) * LANES] = (yb * cos + partner * sin).astype(o_ref.dtype)
        else:
            o_ref[...] = y_rest[:, col - rope_w:col - rope_w + w].astype(o_ref.dtype)
        col += w


def _inproj(x2, g, w, cos_t, sin_t, widths, n_rope, gate_last, seq, tm=512):
    T = x2.shape[0]
    n_pos_blocks = seq // tm
    out_shape = [jax.ShapeDtypeStruct((T, wd), F32 if (gate_last and i == len(widths) - 1) else BF16)
                 for i, wd in enumerate(widths)]
    out_specs = [pl.BlockSpec((tm, wd), lambda m: (m, 0)) for wd in widths]
    return pl.pallas_call(
        functools.partial(_inproj_kernel, widths=tuple(widths), n_rope=n_rope, gate_last=gate_last),
        grid=(T // tm,),
        in_specs=[pl.BlockSpec((tm, D_MODEL), lambda m: (m, 0)),
                  _resident((1, D_MODEL)),
                  _resident(w.shape),
                  pl.BlockSpec((tm, LANES), lambda m: (m % n_pos_blocks, 0)),
                  pl.BlockSpec((tm, LANES), lambda m: (m % n_pos_blocks, 0))],
        out_specs=out_specs,
        out_shape=out_shape,
        compiler_params=_params(("parallel",)),
        name="inproj",
    )(x2, g.reshape(1, D_MODEL), w, cos_t, sin_t)


def _compress_kernel(kc_ref, vc_ref, pk_ref, pv_ref, wk_ref, wv_ref, ko_ref, vo_ref):
    def one(x_ref, p_ref, w_ref, o_ref):
        x = x_ref[...].astype(F32)
        first = _dot((x + p_ref[0:1, :]).astype(BF16), w_ref[0])
        second = _dot((x + p_ref[1:2, :]).astype(BF16), w_ref[1])
        n = second.shape[0]
        o_ref[...] = (first + pltpu.roll(second, n - 1, 0)).astype(o_ref.dtype)

    one(kc_ref, pk_ref, wk_ref, ko_ref)
    one(vc_ref, pv_ref, wv_ref, vo_ref)


def _compress(kc, vc, pk, pv, wk, wv, batch, seq):
    rows = seq // CMP_STRIDE
    width = CMP_STRIDE * LANES
    kc3 = kc.reshape(batch, rows, width)
    vc3 = vc.reshape(batch, rows, width)
    blk = pl.BlockSpec((None, rows, width), lambda b: (b, 0, 0))
    oblk = pl.BlockSpec((None, rows, LANES), lambda b: (b, 0, 0))
    return pl.pallas_call(
        _compress_kernel,
        grid=(batch,),
        in_specs=[blk, blk, _resident(pk.shape), _resident(pv.shape), _resident(wk.shape), _resident(wv.shape)],
        out_specs=[oblk, oblk],
        out_shape=[jax.ShapeDtypeStruct((batch, rows, LANES), BF16)] * 2,
        compiler_params=_params(("parallel",)),
        name="nsa_compress",
    )(kc3, vc3, pk, pv, wk, wv)


def _with_ones(v):
    return jnp.concatenate([v, jnp.ones_like(v)], axis=1)


def _online_steps(scores, values, carries):
    m_new = [jnp.maximum(m, jnp.max(s, axis=-1, keepdims=True)) for s, (m, _) in zip(scores, carries)]
    p = [jnp.exp(s - mn) for s, mn in zip(scores, m_new)]
    pv = [_dot(pi.astype(BF16), v) for pi, v in zip(p, values)]
    return [(mn, jnp.exp(m - mn) * acc + pvi) for pvi, mn, (m, acc) in zip(pv, m_new, carries)]


def _online_init(tq, n):
    return [(jnp.full((tq, 1), NEG_INF, F32), jnp.zeros((tq, 2 * LANES), F32)) for _ in range(n)]


def _online_result(carry):
    _, acc = carry
    return acc[:, :LANES] / acc[:, LANES:]


def _nsa_kernel(q_ref, kcmp_ref, vcmp_ref, ks_ref, vs_ref, kw_ref, vw_ref, gl_ref, ct_ref, e_ref,
                o_ref, *, tq, tk, n_cmp_rows, n_slc):
    qi = pl.program_id(1)
    q0 = qi * tq
    lane = lax.broadcasted_iota(jnp.int32, (1, LANES), 1)
    half = [lane < HEAD_DIM, lane >= HEAD_DIM]
    t_col = q0 + lax.broadcasted_iota(jnp.int32, (tq, 1), 0)

    qm = [[jnp.where(half[g], q_ref[:, i * LANES:(i + 1) * LANES], jnp.zeros((), BF16))
           for i in range(NSA_HPG)] for g in range(NSA_GROUPS)]

    heads = [(g, i) for g in range(NSA_GROUPS) for i in range(NSA_HPG)]
    n_heads = len(heads)

    n_idx = lax.broadcasted_iota(jnp.int32, (1, n_cmp_rows), 1)
    cmp_ok = (n_idx * CMP_STRIDE + (CMP_BLOCK - 1)) <= t_col
    kcmp = kcmp_ref[...]
    vcmp = vcmp_ref[...]
    s_cmp = [_dot_nt(qm[g][i], kcmp) for g, i in heads]
    p_cmp = []
    for s in s_cmp:
        s = jnp.where(cmp_ok, s, NEG_INF)
        e = jnp.where(cmp_ok, jnp.exp(s - jnp.max(s, axis=-1, keepdims=True)), 0.0)
        l = jnp.sum(e, axis=-1, keepdims=True)
        p_cmp.append(e * jnp.where(l > 0.0, 1.0 / l, 0.0))
    o_cmp = [_dot(p.astype(BF16), vcmp) for p in p_cmp]
    p_sum = [sum(p_cmp[g * NSA_HPG + i] for i in range(NSA_HPG)) for g in range(NSA_GROUPS)]

    n_grp = n_slc // SUBLANES
    jb = lax.broadcasted_iota(jnp.int32, (n_slc, 1), 0)
    sub = lax.broadcasted_iota(jnp.int32, (SUBLANES, 1), 0)
    qblk = (q0 + lax.broadcasted_iota(jnp.int32, (1, tq), 1)) // SEL_BLOCK
    forced = (jb == 0) | (jb == qblk) | (jb == qblk - 1)
    future = jb > qblk
    key_in_tile = lax.broadcasted_iota(jnp.int32, (1, tk), 1)
    ct = ct_ref[...]
    dropped = []
    for g in range(NSA_GROUPS):
        r0 = p_sum[g]
        p_hi = r0.astype(BF16)
        r1 = r0 - p_hi.astype(F32)
        p_mid = r1.astype(BF16)
        p_lo = (r1 - p_mid.astype(F32)).astype(BF16)
        imp = _dot_nt(ct, p_hi) + _dot_nt(ct, p_mid) + _dot_nt(ct, p_lo)
        imp = jnp.where(forced, BIG, jnp.where(future, -BIG, imp))
        imp_g = [imp[r * SUBLANES:(r + 1) * SUBLANES, :] for r in range(n_grp)]
        cnt = [jnp.zeros((SUBLANES, tq), F32) for _ in range(n_grp)]
        for j in range(n_slc):
            r, jj = divmod(j, SUBLANES)
            row = imp_g[r][jj:jj + 1, :]
            for b in range(n_grp):
                if b < r:
                    inc = jnp.where(row > imp_g[b], 1.0, 0.0)
                elif b > r:
                    inc = jnp.where(row >= imp_g[b], 1.0, 0.0)
                else:
                    tie = jnp.where(sub > jj, 1.0, 0.0)
                    inc = jnp.where(row > imp_g[b], 1.0, jnp.where(row == imp_g[b], tie, 0.0))
                cnt[b] = cnt[b] + inc
        drop_t = [jnp.where((cnt[b] < float(SEL_TOPK)) & (imp_g[b] >= 0.0), 0.0, 1.0) for b in range(n_grp)]
        drop_t = jnp.concatenate(drop_t + [jnp.ones((LANES - n_slc, tq), F32)], axis=0)
        dropped.append(drop_t.T.astype(BF16))

    q_sel = [jnp.concatenate([qm[g][i], dropped[g]], axis=1) for g, i in heads]

    def sel_tile(kt, carry, masked=False):
        start = pl.multiple_of(kt * tk, tk)
        k = jnp.concatenate([ks_ref[pl.ds(start, tk), :], e_ref[pl.ds(start, tk), :]], axis=1)
        v = _with_ones(vs_ref[pl.ds(start, tk), :])
        s = [_dot_nt(q, k) for q in q_sel]
        if masked:
            ok = (kt * tk + key_in_tile) <= t_col
            s = [jnp.where(ok, x, NEG_INF) for x in s]
        return _online_steps(s, [v] * n_heads, carry)

    n_kt = (q0 + tq + tk - 1) // tk
    carry = lax.fori_loop(0, n_kt - 1, sel_tile, _online_init(tq, n_heads))
    o_sel = [_online_result(c) for c in sel_tile(n_kt - 1, carry, True)]

    span = WINDOW + tq
    start = pl.multiple_of(jnp.maximum(q0 - WINDOW, 0), tq)
    key = start + lax.broadcasted_iota(jnp.int32, (1, span), 1)
    ok = (key <= t_col) & (key > t_col - WINDOW)
    k = kw_ref[pl.ds(start, span), :]
    v = _with_ones(vw_ref[pl.ds(start, span), :])
    o_win = []
    for g in range(NSA_GROUPS):
        s_win = [jnp.where(ok, _dot_nt(qm[g][i], k), NEG_INF) for i in range(NSA_HPG)]
        p_win = [jnp.exp(s - jnp.max(s, axis=-1, keepdims=True)) for s in s_win]
        pv_win = [_dot(p.astype(BF16), v) for p in p_win]
        o_win += [pv[:, :LANES] / pv[:, LANES:] for pv in pv_win]

    gates = jax.nn.sigmoid(gl_ref[...])
    for i in range(NSA_HPG):
        outs = []
        for g in range(NSA_GROUPS):
            hd = g * NSA_HPG + i
            outs.append(gates[:, 3 * hd:3 * hd + 1] * o_cmp[hd]
                        + gates[:, 3 * hd + 1:3 * hd + 2] * o_sel[hd]
                        + gates[:, 3 * hd + 2:3 * hd + 3] * o_win[hd])
        o_ref[:, i * LANES:(i + 1) * LANES] = jnp.where(half[0], outs[0], outs[1]).astype(o_ref.dtype)


def _nsa(qn, kcmp, vcmp, ks, vs, kw, vw, gl, ct, e_mat, batch, seq, tq, tk):
    nq = seq // tq
    n_cmp_rows = seq // CMP_STRIDE
    n_slc = seq // SEL_BLOCK
    row_blk = lambda w: pl.BlockSpec((tq, w), lambda b, q: (b * nq + q, 0))
    seq_blk = pl.BlockSpec((seq, LANES), lambda b, q: (b, 0))
    cmp_blk = pl.BlockSpec((None, n_cmp_rows, LANES), lambda b, q: (b, 0, 0))
    return pl.pallas_call(
        functools.partial(_nsa_kernel, tq=tq, tk=tk, n_cmp_rows=n_cmp_rows, n_slc=n_slc),
        grid=(batch, nq),
        in_specs=[row_blk(NSA_Q_W), cmp_blk, cmp_blk, seq_blk, seq_blk, seq_blk, seq_blk, row_blk(LANES),
                  _resident(ct.shape), _resident(e_mat.shape)],
        out_specs=row_blk(NSA_Q_W),
        out_shape=jax.ShapeDtypeStruct((batch * seq, NSA_Q_W), BF16),
        compiler_params=_params(("parallel", "arbitrary")),
        name="nsa_attention",
    )(qn, kcmp, vcmp, ks, vs, kw, vw, gl, ct, e_mat)


def _diff_kernel(q_ref, k_ref, v_ref, lam_ref, sub_ref, o_ref, *, tq, tk, n_heads, lam_init):
    qi = pl.program_id(2)
    n_diag = max(tq // tk, 1)
    lane = lax.broadcasted_iota(jnp.int32, (1, LANES), 1)
    row = lax.broadcasted_iota(jnp.int32, (tq, 1), 0)
    key_in_tile = lax.broadcasted_iota(jnp.int32, (1, tk), 1)
    zero = jnp.zeros((), BF16)
    qs = []
    for h in range(n_heads):
        q = q_ref[:, h * LANES:(h + 1) * LANES]
        qs += [jnp.where(lane < HEAD_DIM, q, zero), jnp.where(lane >= HEAD_DIM, q, zero)]

    def tile(kt, carry, masked=False):
        start = pl.multiple_of(kt * tk, tk)
        k = [k_ref[pl.ds(start, tk), h * LANES:(h + 1) * LANES] for h in range(n_heads)]
        v = [_with_ones(v_ref[pl.ds(start, tk), h * LANES:(h + 1) * LANES]) for h in range(n_heads)]
        s = [_dot_nt(qs[c], k[c // 2]) for c in range(2 * n_heads)]
        if masked:
            ok = (kt * tk + key_in_tile) <= (qi * tq + row)
            s = [jnp.where(ok, x, NEG_INF) for x in s]
        return _online_steps(s, [v[c // 2] for c in range(2 * n_heads)], carry)

    n_full = (qi * tq) // tk
    carry = lax.fori_loop(0, n_full, tile, _online_init(tq, 2 * n_heads))
    for j in range(n_diag):
        carry = tile(n_full + j, carry, True)

    lam_rows = lam_ref[...]
    lam = (jnp.exp(jnp.sum(lam_rows[0:1] * lam_rows[1:2], axis=-1, keepdims=True))
           - jnp.exp(jnp.sum(lam_rows[2:3] * lam_rows[3:4], axis=-1, keepdims=True)) + lam_init)
    for h in range(n_heads):
        o = _online_result(carry[2 * h]) - lam * _online_result(carry[2 * h + 1])
        o_ref[:, h * LANES:(h + 1) * LANES] = (_rms(o, sub_ref[...]) * (1.0 - lam_init)).astype(o_ref.dtype)


def _diff(dq, dk, dv, lam_rows, subln, lam_init, batch, seq, tq, tk, n_heads=4):
    nq = seq // tq
    w = n_heads * LANES
    return pl.pallas_call(
        functools.partial(_diff_kernel, tq=tq, tk=tk, n_heads=n_heads, lam_init=lam_init),
        grid=(batch, DIFF_HEADS // n_heads, nq),
        in_specs=[pl.BlockSpec((tq, w), lambda b, h, q: (b * nq + q, h)),
                  pl.BlockSpec((seq, w), lambda b, h, q: (b, h)),
                  pl.BlockSpec((seq, w), lambda b, h, q: (b, h)),
                  _resident(lam_rows.shape), _resident(subln.shape)],
        out_specs=pl.BlockSpec((tq, w), lambda b, h, q: (b * nq + q, h)),
        out_shape=jax.ShapeDtypeStruct((batch * seq, DIFF_V_W), BF16),
        compiler_params=_params(("parallel", "parallel", "arbitrary")),
        name="diff_attention",
    )(dq, dk, dv, lam_rows, subln)


def _sb_kernel(q_ref, k_ref, v_ref, u_ref, o_ref, *, tq, tk, n_pairs):
    qi = pl.program_id(2)
    n_diag = tq // tk
    n_ch = 2 * n_pairs
    lane = lax.broadcasted_iota(jnp.int32, (1, LANES), 1)
    row = lax.broadcasted_iota(jnp.int32, (tq, 1), 0)
    key_in_tile = lax.broadcasted_iota(jnp.int32, (1, tk), 1)
    zero = jnp.zeros((), BF16)
    qa = []
    for p in range(n_pairs):
        q = q_ref[:, p * LANES:(p + 1) * LANES]
        qa += [jnp.where(lane < HEAD_DIM, q, zero), jnp.where(lane >= HEAD_DIM, q, zero)]
    u2 = u_ref[...]

    def rows(ref, kt, p):
        return ref[pl.ds(pl.multiple_of(kt * tk, tk), tk), p * LANES:(p + 1) * LANES]

    def sweep(kts, carry, stricts=None):
        c, acc = carry
        chains = [(i, h) for i in range(len(kts)) for h in range(n_ch)]
        strict = {i: None if stricts is None else stricts[i] for i in range(len(kts))}
        z = {}
        for i, kt in enumerate(kts):
            k = [rows(k_ref, kt, p) for p in range(n_pairs)]
            for h in range(n_ch):
                z[i, h] = _dot_nt(qa[h], k[h // 2])
        nlr, pieces = {}, {}
        for ch in chains:
            n = jnp.where(z[ch] > SOFTPLUS_LINEAR, z[ch], jnp.log(1.0 + jnp.exp(z[ch])))
            if strict[ch[0]] is not None:
                n = jnp.where(strict[ch[0]], n, 0.0)
            hi = n.astype(BF16)
            lo = (n - hi.astype(F32)).astype(BF16)
            nlr[ch] = n
            pieces[ch] = jnp.concatenate([hi, lo], axis=1)
        nsuf = {ch: _dot(pieces[ch], u2) for ch in chains}
        w = {}
        offset = list(c)
        for i in range(len(kts)):
            for h in range(n_ch):
                wa = jnp.exp(z[i, h] - nlr[i, h] - nsuf[i, h] - offset[h])
                if strict[i] is not None:
                    wa = jnp.where(strict[i], wa, 0.0)
                w[i, h] = wa.astype(BF16)
                offset[h] = offset[h] + nsuf[i, h][:, 0:1] + nlr[i, h][:, 0:1]
        v = {(i, p): rows(v_ref, kt, p) for i, kt in enumerate(kts) for p in range(n_pairs)}
        pv = {ch: _dot(w[ch], v[ch[0], ch[1] // 2]) for ch in chains}
        acc = tuple(acc[h] + sum(pv[i, h] for i in range(len(kts))) for h in range(n_ch))
        return tuple(offset), acc

    def smallest(c):
        return functools.reduce(jnp.minimum, [jnp.min(x) for x in c])

    carry = (tuple(jnp.zeros((tq, 1), F32) for _ in range(n_ch)),
             tuple(jnp.zeros((tq, LANES), F32) for _ in range(n_ch)))
    diag = list(reversed(range(n_diag)))
    carry = sweep([qi * n_diag + j for j in diag], carry, [(j * tk + key_in_tile) < row for j in diag])
    n_off = qi * n_diag

    def more(state):
        r, c_min, _ = state
        return jnp.logical_and(r < n_off, c_min < SB_UNDERFLOW)

    def step(state):
        r, _, cr = state
        cr = sweep([n_off - 1 - r], cr)
        return r + 1, smallest(cr[0]), cr

    _, _, (_, acc) = lax.while_loop(more, step, (jnp.int32(0), smallest(carry[0]), carry))
    for p in range(n_pairs):
        o_ref[:, p * LANES:(p + 1) * LANES] = jnp.where(
            lane < HEAD_DIM, acc[2 * p], acc[2 * p + 1]).astype(o_ref.dtype)


def _sb(qkv, u_mat, batch, seq, tq, tk, n_pairs):
    nq = seq // tq
    n_blocks = SB_HEADS // 2 // n_pairs
    w = n_pairs * LANES
    return pl.pallas_call(
        functools.partial(_sb_kernel, tq=tq, tk=tk, n_pairs=n_pairs),
        grid=(batch, n_blocks, nq),
        in_specs=[pl.BlockSpec((tq, w), lambda b, p, q: (b * nq + q, p)),
                  pl.BlockSpec((seq, w), lambda b, p, q: (b, n_blocks + p)),
                  pl.BlockSpec((seq, w), lambda b, p, q: (b, 2 * n_blocks + p)),
                  _resident(u_mat.shape)],
        out_specs=pl.BlockSpec((tq, w), lambda b, p, q: (b * nq + q, p)),
        out_shape=jax.ShapeDtypeStruct((batch * seq, D_MODEL), BF16),
        compiler_params=_params(("parallel", "parallel", "arbitrary")),
        name="sb_attention",
    )(qkv, qkv, qkv, u_mat)


def _mix_ffn_kernel(x_ref, oa_ref, ob_ref, wo_ref, g_ref, wg_ref, wu_ref, wd_ref, gf_ref, out_ref, *,
                    final_norm):
    half = oa_ref.shape[1]
    x = x_ref[...] + _dot(oa_ref[...], wo_ref[0:half, :]) + _dot(ob_ref[...], wo_ref[half:2 * half, :])
    h = _rms(x, g_ref[...]).astype(BF16)
    gate = _dot(h, wg_ref[...])
    up = _dot(h, wu_ref[...])
    act = (gate * jax.nn.sigmoid(gate) * up).astype(BF16)
    acc = x + _dot(act, wd_ref[...])
    if final_norm:
        acc = _rms(acc, gf_ref[...])
    out_ref[...] = acc


def _mix_ffn(x2, oa, ob, oa_blk, ob_blk, wo, g, wg, wu, wd, gf, final_norm, tm=512):
    T = x2.shape[0]
    half = D_MODEL // 2
    return pl.pallas_call(
        functools.partial(_mix_ffn_kernel, final_norm=final_norm),
        grid=(T // tm,),
        in_specs=[pl.BlockSpec((tm, D_MODEL), lambda m: (m, 0)),
                  pl.BlockSpec((tm, half), lambda m: (m, oa_blk)),
                  pl.BlockSpec((tm, half), lambda m: (m, ob_blk)),
                  _resident(wo.shape), _resident((1, D_MODEL)),
                  _resident(wg.shape), _resident(wu.shape), _resident(wd.shape), _resident((1, D_MODEL))],
        out_specs=pl.BlockSpec((tm, D_MODEL), lambda m: (m, 0)),
        out_shape=jax.ShapeDtypeStruct((T, D_MODEL), F32),
        compiler_params=_params(("parallel",)),
        name="outproj_ffn",
    )(x2, oa, ob, wo, g.reshape(1, D_MODEL), wg, wu, wd, gf.reshape(1, D_MODEL))


def _cast_kernel(w_ref, o_ref):
    o_ref[...] = w_ref[...].astype(o_ref.dtype)


def _to_bf16(w, layer, rows=256):
    _, n, c = w.shape
    return pl.pallas_call(
        _cast_kernel,
        grid=(n // rows,),
        in_specs=[pl.BlockSpec((None, rows, c), lambda i: (layer, i, 0))],
        out_specs=pl.BlockSpec((rows, c), lambda i: (i, 0)),
        out_shape=jax.ShapeDtypeStruct((n, c), BF16),
        compiler_params=_params(("parallel",)),
        name="cast_bf16",
    )(w)


def _even_in_columns():
    off = np.cumsum([0, NSA_Q_W, NSA_KV_W, NSA_KV_W, NSA_KV_W, NSA_KV_W, NSA_KV_W, NSA_KV_W,
                     NSA_GATE_W, DIFF_QK_W, DIFF_QK_W, DIFF_V_W])
    nq, kc, vc, ks, vs, kw, vw, gl, dq, dk, dv = off[:-1]
    rng = lambda s, w: np.arange(s, s + w)
    q_cols = np.concatenate([np.concatenate([rng(nq + HEAD_DIM * i, HEAD_DIM),
                                             rng(nq + HEAD_DIM * (NSA_HPG + i), HEAD_DIM)])
                             for i in range(NSA_HPG)])
    pieces = [(q_cols, True), (rng(kc, NSA_KV_W), True), (rng(ks, NSA_KV_W), True), (rng(kw, NSA_KV_W), True),
              (rng(dq, DIFF_QK_W), True), (rng(dk, DIFF_QK_W), True),
              (rng(vc, NSA_KV_W), False), (rng(vs, NSA_KV_W), False), (rng(vw, NSA_KV_W), False),
              (rng(dv, DIFF_V_W), False), (rng(gl, NSA_GATE_W), False)]
    return pieces


def _prep_even_w_in(w):
    pieces = _even_in_columns()
    scale = HEAD_DIM ** -0.5
    cols, widths = [], []
    for idx, (src, _) in enumerate(pieces):
        blk = w[:, src]
        if idx in (0, 4):
            blk = blk * scale
        pad = (-blk.shape[1]) % LANES
        if pad:
            blk = jnp.pad(blk, ((0, 0), (0, pad)))
        cols.append(blk)
        widths.append(blk.shape[1])
    n_rope = sum(1 for _, r in pieces if r)
    return jnp.concatenate(cols, axis=1).astype(BF16), widths, n_rope


def _nsa_out_rows():
    rows = []
    for i in range(NSA_HPG):
        for g in range(NSA_GROUPS):
            hd = g * NSA_HPG + i
            rows.append(np.arange(hd * HEAD_DIM, (hd + 1) * HEAD_DIM))
    return np.concatenate(rows + [np.arange(NSA_Q_W, NSA_Q_W + DIFF_V_W)])


def _prep_cmp(pos, w):
    half = CMP_BLOCK // 2
    p2 = jnp.tile(pos.reshape(2, half, 1, HEAD_DIM), (1, 1, NSA_GROUPS, 1)).reshape(2, half * LANES)
    w4 = w.reshape(2, half, HEAD_DIM, HEAD_DIM).astype(BF16)
    zeros = jnp.zeros_like(w4)
    wbd = jnp.stack([jnp.concatenate([w4 if c == g else zeros for c in range(NSA_GROUPS)], axis=-1)
                     for g in range(NSA_GROUPS)], axis=2)
    return p2.astype(F32), wbd.reshape(2, half * LANES, LANES)


def _rope_tables(seq):
    inv = 1.0 / (ROPE_THETA ** (jnp.arange(0, HEAD_DIM, 2, dtype=F32) / HEAD_DIM))
    ang = jnp.arange(seq, dtype=F32)[:, None] * inv[None, :]
    cos, sin = jnp.cos(ang), jnp.sin(ang)
    reps = LANES // HEAD_DIM
    cos_t = jnp.tile(jnp.concatenate([cos, cos], axis=1), (1, reps))
    sin_t = jnp.tile(jnp.concatenate([-sin, sin], axis=1), (1, reps))
    return cos_t, sin_t


def _selection_constants(seq, sb_tk):
    n_cmp_rows = seq // CMP_STRIDE
    n_slc = seq // SEL_BLOCK
    n = np.arange(n_cmp_rows)[None, :] * CMP_STRIDE
    j = np.arange(n_slc)[:, None] * SEL_BLOCK
    ct = ((n < j + SEL_BLOCK) & (n + CMP_BLOCK > j)).astype(np.float32)
    e = np.zeros((seq, LANES), np.float32)
    e[np.arange(seq), np.arange(seq) // SEL_BLOCK] = NEG_INF
    u =(np.arange(sb_tk)[:, None] > np.arange(sb_tk)[None, :]).astype(np.float32)
    return jnp.asarray(ct, BF16), jnp.asarray(e, BF16), jnp.asarray(np.concatenate([u, u], axis=0), BF16)


def kernel(x, norm_mix, norm_ffn, norm_final, even_w_in, even_w_out, cmp_pos_k, cmp_w_k, cmp_pos_v, cmp_w_v,
           diff_lq1, diff_lk1, diff_lq2, diff_lk2, diff_subln, odd_w_in, odd_w_out, ffn_w_gate, ffn_w_up,
           ffn_w_down):
    batch, seq, _ = x.shape
    nsa_tq, nsa_tk = 256, 512
    sb_tq, sb_tk, sb_pairs = 256, 256, 4
    diff_tq, diff_tk = 256, 1024
    assert seq % sb_tq == 0 and seq % diff_tk == 0 and seq // SEL_BLOCK <= LANES and seq // SEL_BLOCK >= SEL_TOPK
    cos_t, sin_t = _rope_tables(seq)
    ct, e_mat, u_mat = _selection_constants(seq, sb_tk)
    out_rows = _nsa_out_rows()
    sb_scale = jnp.concatenate([jnp.full((D_MODEL,), HEAD_DIM ** -0.5, F32), jnp.ones((2 * D_MODEL,), F32)])

    x2 = x.reshape(batch * seq, D_MODEL)
    for layer in range(DEPTH):
        last = layer == DEPTH - 1
        if layer % 2 == 0:
            e = layer // 2
            lam_init = 0.8 - 0.6 * math.exp(-0.3 * layer)
            w_in, widths, n_rope = _prep_even_w_in(even_w_in[e])
            qn, kc, ks, kw, dq, dk, vc, vs, vw, dv, gl = _inproj(
                x2, norm_mix[layer], w_in, cos_t, sin_t, widths, n_rope, True, seq)
            pk, wk = _prep_cmp(cmp_pos_k[e], cmp_w_k[e])
            pv, wv = _prep_cmp(cmp_pos_v[e], cmp_w_v[e])
            kcmp, vcmp = _compress(kc, vc, pk, pv, wk, wv, batch, seq)
            o_nsa = _nsa(qn, kcmp, vcmp, ks, vs, kw, vw, gl, ct, e_mat, batch, seq, nsa_tq, nsa_tk)
            lam_rows = jnp.zeros((8, LANES), F32).at[0:4, 0:HEAD_DIM].set(
                jnp.stack([diff_lq1[e], diff_lk1[e], diff_lq2[e], diff_lk2[e]]))
            o_diff = _diff(dq, dk, dv, lam_rows, diff_subln[e].reshape(1, DIFF_VDIM), lam_init, batch, seq,
                           diff_tq, diff_tk)
            wo = even_w_out[e][out_rows].astype(BF16)
            oa, ob, oa_blk, ob_blk = o_nsa, o_diff, 0, 0
        else:
            o = layer // 2
            w_in = (odd_w_in[o] * sb_scale[None, :]).astype(BF16)
            (qkv,) = _inproj(x2, norm_mix[layer], w_in, cos_t, sin_t, [3 * D_MODEL], 0, False, seq)
            o_sb = _sb(qkv, u_mat, batch, seq, sb_tq, sb_tk, sb_pairs)
            wo = odd_w_out[o].astype(BF16)
            oa, ob, oa_blk, ob_blk = o_sb, o_sb, 0, 1
        x2 = _mix_ffn(x2, oa, ob, oa_blk, ob_blk, wo, norm_ffn[layer],
                      _to_bf16(ffn_w_gate, layer), _to_bf16(ffn_w_up, layer),
                      _to_bf16(ffn_w_down, layer), norm_final, last)
    return x2.reshape(batch, seq, D_MODEL)
```

```python
import functools
import math

import numpy as np
import jax
import jax.numpy as jnp
from jax import lax
from jax.experimental import pallas as pl
from jax.experimental.pallas import tpu as pltpu

D_MODEL = 1024
DEPTH = 4
HEAD_DIM = 64
ROPE_THETA = 10000.0
NORM_EPS = 1e-6
NEG_INF = -1e30
BIG = 1e30
SOFTPLUS_LINEAR = 40.0
SB_UNDERFLOW = 110.0

NSA_HEADS = 8
NSA_GROUPS = 2
NSA_HPG = NSA_HEADS // NSA_GROUPS
CMP_BLOCK = 32
CMP_STRIDE = 16
SEL_BLOCK = 64
SEL_TOPK = 16
WINDOW = 512

DIFF_HEADS = 4
DIFF_VDIM = 2 * HEAD_DIM

SB_HEADS = D_MODEL // HEAD_DIM
FFN_HIDDEN = -(-(8 * D_MODEL) // (3 * 256)) * 256

NSA_Q_W = NSA_HEADS * HEAD_DIM
NSA_KV_W = NSA_GROUPS * HEAD_DIM
NSA_GATE_W = 3 * NSA_HEADS
DIFF_QK_W = DIFF_HEADS * 2 * HEAD_DIM
DIFF_V_W = DIFF_HEADS * DIFF_VDIM

LANES = 128
SUBLANES = 8
VMEM_LIMIT = 56 * 1024 * 1024

BF16 = jnp.bfloat16
F32 = jnp.float32


def _dot(a, b):
    return jnp.dot(a, b, preferred_element_type=F32)


def _dot_nt(a, b):
    return lax.dot_general(a, b, (((1,), (1,)), ((), ())), preferred_element_type=F32)


def _rms(x, g):
    return x * lax.rsqrt(jnp.mean(x * x, axis=-1, keepdims=True) + NORM_EPS) * g


def _params(sem, vmem=VMEM_LIMIT):
    return pltpu.CompilerParams(dimension_semantics=sem, vmem_limit_bytes=vmem)


def _resident(shape):
    nd = len(shape)
    return pl.BlockSpec(shape, lambda *_: (0,) * nd, pipeline_mode=pl.Buffered(1))


def _inproj_kernel(x_ref, g_ref, w_ref, cos_ref, sin_ref, *out_refs, widths, n_rope, gate_last):
    h = _rms(x_ref[...], g_ref[...]).astype(BF16)
    lane = lax.broadcasted_iota(jnp.int32, (1, LANES), 1)
    first_half = (lane % HEAD_DIM) < (HEAD_DIM // 2)
    rope_w = sum(widths[:n_rope])
    total_w = sum(widths)
    y_rope = _dot(h, w_ref[:, 0:rope_w]) if n_rope else None
    y_rest = _dot(h, w_ref[:, rope_w:total_w])
    col = 0
    for oi, (o_ref, w) in enumerate(zip(out_refs, widths)):
        if oi < n_rope:
            cos, sin = cos_ref[...], sin_ref[...]
            for b in range(w // LANES):
                yb = y_rope[:, col + b * LANES:col + (b + 1) * LANES]
                partner = jnp.where(first_half, pltpu.roll(yb, LANES - HEAD_DIM // 2, 1),
                                    pltpu.roll(yb, HEAD_DIM // 2, 1))
                o_ref[:, b * LANES:(b + 1) * LANES] = (yb * cos + partner * sin).astype(o_ref.dtype)
        else:
            o_ref[...] = y_rest[:, col - rope_w:col - rope_w + w].astype(o_ref.dtype)
        col += w


def _inproj(x2, g, w, cos_t, sin_t, widths, n_rope, gate_last, seq, tm=512):
    T = x2.shape[0]
    n_pos_blocks = seq // tm
    out_shape = [jax.ShapeDtypeStruct((T, wd), F32 if (gate_last and i == len(widths) - 1) else BF16)
                 for i, wd in enumerate(widths)]
    out_specs = [pl.BlockSpec((tm, wd), lambda m: (m, 0)) for wd in widths]
    return pl.pallas_call(
        functools.partial(_inproj_kernel, widths=tuple(widths), n_rope=n_rope, gate_last=gate_last),
        grid=(T // tm,),
        in_specs=[pl.BlockSpec((tm, D_MODEL), lambda m: (m, 0)),
                  _resident((1, D_MODEL)),
                  _resident(w.shape),
                  pl.BlockSpec((tm, LANES), lambda m: (m % n_pos_blocks, 0)),
                  pl.BlockSpec((tm, LANES), lambda m: (m % n_pos_blocks, 0))],
        out_specs=out_specs,
        out_shape=out_shape,
        compiler_params=_params(("parallel",)),
        name="inproj",
    )(x2, g.reshape(1, D_MODEL), w, cos_t, sin_t)


def _compress_kernel(kc_ref, vc_ref, pk_ref, pv_ref, wk_ref, wv_ref, ko_ref, vo_ref):
    def one(x_ref, p_ref, w_ref, o_ref):
        x = x_ref[...].astype(F32)
        first = _dot((x + p_ref[0:1, :]).astype(BF16), w_ref[0])
        second = _dot((x + p_ref[1:2, :]).astype(BF16), w_ref[1])
        n = second.shape[0]
        o_ref[...] = (first + pltpu.roll(second, n - 1, 0)).astype(o_ref.dtype)

    one(kc_ref, pk_ref, wk_ref, ko_ref)
    one(vc_ref, pv_ref, wv_ref, vo_ref)


def _compress(kc, vc, pk, pv, wk, wv, batch, seq):
    rows = seq // CMP_STRIDE
    width = CMP_STRIDE * LANES
    kc3 = kc.reshape(batch, rows, width)
    vc3 = vc.reshape(batch, rows, width)
    blk = pl.BlockSpec((None, rows, width), lambda b: (b, 0, 0))
    oblk = pl.BlockSpec((None, rows, LANES), lambda b: (b, 0, 0))
    return pl.pallas_call(
        _compress_kernel,
        grid=(batch,),
        in_specs=[blk, blk, _resident(pk.shape), _resident(pv.shape), _resident(wk.shape), _resident(wv.shape)],
        out_specs=[oblk, oblk],
        out_shape=[jax.ShapeDtypeStruct((batch, rows, LANES), BF16)] * 2,
        compiler_params=_params(("parallel",)),
        name="nsa_compress",
    )(kc3, vc3, pk, pv, wk, wv)


def _with_ones(v):
    return jnp.concatenate([v, jnp.ones_like(v)], axis=1)


def _online_steps(scores, values, carries):
    m_new = [jnp.maximum(m, jnp.max(s, axis=-1, keepdims=True)) for s, (m, _) in zip(scores, carries)]
    p = [jnp.exp(s - mn) for s, mn in zip(scores, m_new)]
    pv = [_dot(pi.astype(BF16), v) for pi, v in zip(p, values)]
    return [(mn, jnp.exp(m - mn) * acc + pvi) for pvi, mn, (m, acc) in zip(pv, m_new, carries)]


def _online_init(tq, n):
    return [(jnp.full((tq, 1), NEG_INF, F32), jnp.zeros((tq, 2 * LANES), F32)) for _ in range(n)]


def _online_result(carry):
    _, acc = carry
    return acc[:, :LANES] / acc[:, LANES:]


def _nsa_kernel(q_ref, kcmp_ref, vcmp_ref, ks_ref, vs_ref, kw_ref, vw_ref, gl_ref, ct_ref, e_ref,
                o_ref, *, tq, tk, n_cmp_rows, n_slc):
    qi = pl.program_id(1)
    q0 = qi * tq
    lane = lax.broadcasted_iota(jnp.int32, (1, LANES), 1)
    half = [lane < HEAD_DIM, lane >= HEAD_DIM]
    t_col = q0 + lax.broadcasted_iota(jnp.int32, (tq, 1), 0)

    qm = [[jnp.where(half[g], q_ref[:, i * LANES:(i + 1) * LANES], jnp.zeros((), BF16))
           for i in range(NSA_HPG)] for g in range(NSA_GROUPS)]

    heads = [(g, i) for g in range(NSA_GROUPS) for i in range(NSA_HPG)]
    n_heads = len(heads)

    n_idx = lax.broadcasted_iota(jnp.int32, (1, n_cmp_rows), 1)
    cmp_ok = (n_idx * CMP_STRIDE + (CMP_BLOCK - 1)) <= t_col
    kcmp = kcmp_ref[...]
    vcmp = vcmp_ref[...]
    s_cmp = [_dot_nt(qm[g][i], kcmp) for g, i in heads]
    p_cmp = []
    for s in s_cmp:
        s = jnp.where(cmp_ok, s, NEG_INF)
        e = jnp.where(cmp_ok, jnp.exp(s - jnp.max(s, axis=-1, keepdims=True)), 0.0)
        l = jnp.sum(e, axis=-1, keepdims=True)
        p_cmp.append(e * jnp.where(l > 0.0, 1.0 / l, 0.0))
    o_cmp = [_dot(p.astype(BF16), vcmp) for p in p_cmp]
    p_sum = [sum(p_cmp[g * NSA_HPG + i] for i in range(NSA_HPG)) for g in range(NSA_GROUPS)]

    n_grp = n_slc // SUBLANES
    jb = lax.broadcasted_iota(jnp.int32, (n_slc, 1), 0)
    sub = lax.broadcasted_iota(jnp.int32, (SUBLANES, 1), 0)
    qblk = (q0 + lax.broadcasted_iota(jnp.int32, (1, tq), 1)) // SEL_BLOCK
    forced = (jb == 0) | (jb == qblk) | (jb == qblk - 1)
    future = jb > qblk
    key_in_tile = lax.broadcasted_iota(jnp.int32, (1, tk), 1)
    ct = ct_ref[...]
    dropped = []
    for g in range(NSA_GROUPS):
        r0 = p_sum[g]
        p_hi = r0.astype(BF16)
        r1 = r0 - p_hi.astype(F32)
        p_mid = r1.astype(BF16)
        p_lo = (r1 - p_mid.astype(F32)).astype(BF16)
        imp = _dot_nt(ct, p_hi) + _dot_nt(ct, p_mid) + _dot_nt(ct, p_lo)
        imp = jnp.where(forced, BIG, jnp.where(future, -BIG, imp))
        imp_g = [imp[r * SUBLANES:(r + 1) * SUBLANES, :] for r in range(n_grp)]
        cnt = [jnp.zeros((SUBLANES, tq), F32) for _ in range(n_grp)]
        for j in range(n_slc):
            r, jj = divmod(j, SUBLANES)
            row = imp_g[r][jj:jj + 1, :]
            for b in range(n_grp):
                if b < r:
                    inc = jnp.where(row > imp_g[b], 1.0, 0.0)
                elif b > r:
                    inc = jnp.where(row >= imp_g[b], 1.0, 0.0)
                else:
                    tie = jnp.where(sub > jj, 1.0, 0.0)
                    inc = jnp.where(row > imp_g[b], 1.0, jnp.where(row == imp_g[b], tie, 0.0))
                cnt[b] = cnt[b] + inc
        drop_t = [jnp.where((cnt[b] < float(SEL_TOPK)) & (imp_g[b] >= 0.0), 0.0, 1.0) for b in range(n_grp)]
        drop_t = jnp.concatenate(drop_t + [jnp.ones((LANES - n_slc, tq), F32)], axis=0)
        dropped.append(drop_t.T.astype(BF16))

    q_sel = [jnp.concatenate([qm[g][i], dropped[g]], axis=1) for g, i in heads]

    def sel_tile(kt, carry, masked=False):
        start = pl.multiple_of(kt * tk, tk)
        k = jnp.concatenate([ks_ref[pl.ds(start, tk), :], e_ref[pl.ds(start, tk), :]], axis=1)
        v = _with_ones(vs_ref[pl.ds(start, tk), :])
        s = [_dot_nt(q, k) for q in q_sel]
        if masked:
            ok = (kt * tk + key_in_tile) <= t_col
            s = [jnp.where(ok, x, NEG_INF) for x in s]
        return _online_steps(s, [v] * n_heads, carry)

    n_kt = (q0 + tq + tk - 1) // tk
    carry = lax.fori_loop(0, n_kt - 1, sel_tile, _online_init(tq, n_heads))
    o_sel = [_online_result(c) for c in sel_tile(n_kt - 1, carry, True)]

    span = WINDOW + tq
    start = pl.multiple_of(jnp.maximum(q0 - WINDOW, 0), tq)
    key = start + lax.broadcasted_iota(jnp.int32, (1, span), 1)
    ok = (key <= t_col) & (key > t_col - WINDOW)
    k = kw_ref[pl.ds(start, span), :]
    v = _with_ones(vw_ref[pl.ds(start, span), :])
    o_win = []
    for g in range(NSA_GROUPS):
        s_win = [jnp.where(ok, _dot_nt(qm[g][i], k), NEG_INF) for i in range(NSA_HPG)]
        p_win = [jnp.exp(s - jnp.max(s, axis=-1, keepdims=True)) for s in s_win]
        pv_win = [_dot(p.astype(BF16), v) for p in p_win]
        o_win += [pv[:, :LANES] / pv[:, LANES:] for pv in pv_win]

    gates = jax.nn.sigmoid(gl_ref[...])
    for i in range(NSA_HPG):
        outs = []
        for g in range(NSA_GROUPS):
            hd = g * NSA_HPG + i
            outs.append(gates[:, 3 * hd:3 * hd + 1] * o_cmp[hd]
                        + gates[:, 3 * hd + 1:3 * hd + 2] * o_sel[hd]
                        + gates[:, 3 * hd + 2:3 * hd + 3] * o_win[hd])
        o_ref[:, i * LANES:(i + 1) * LANES] = jnp.where(half[0], outs[0], outs[1]).astype(o_ref.dtype)


def _nsa(qn, kcmp, vcmp, ks, vs, kw, vw, gl, ct, e_mat, batch, seq, tq, tk):
    nq = seq // tq
    n_cmp_rows = seq // CMP_STRIDE
    n_slc = seq // SEL_BLOCK
    row_blk = lambda w: pl.BlockSpec((tq, w), lambda b, q: (b * nq + q, 0))
    seq_blk = pl.BlockSpec((seq, LANES), lambda b, q: (b, 0))
    cmp_blk = pl.BlockSpec((None, n_cmp_rows, LANES), lambda b, q: (b, 0, 0))
    return pl.pallas_call(
        functools.partial(_nsa_kernel, tq=tq, tk=tk, n_cmp_rows=n_cmp_rows, n_slc=n_slc),
        grid=(batch, nq),
        in_specs=[row_blk(NSA_Q_W), cmp_blk, cmp_blk, seq_blk, seq_blk, seq_blk, seq_blk, row_blk(LANES),
                  _resident(ct.shape), _resident(e_mat.shape)],
        out_specs=row_blk(NSA_Q_W),
        out_shape=jax.ShapeDtypeStruct((batch * seq, NSA_Q_W), BF16),
        compiler_params=_params(("parallel", "arbitrary")),
        name="nsa_attention",
    )(qn, kcmp, vcmp, ks, vs, kw, vw, gl, ct, e_mat)


def _diff_kernel(q_ref, k_ref, v_ref, lam_ref, sub_ref, o_ref, *, tq, tk, n_heads, lam_init):
    qi = pl.program_id(2)
    n_diag = max(tq // tk, 1)
    lane = lax.broadcasted_iota(jnp.int32, (1, LANES), 1)
    row = lax.broadcasted_iota(jnp.int32, (tq, 1), 0)
    key_in_tile = lax.broadcasted_iota(jnp.int32, (1, tk), 1)
    zero = jnp.zeros((), BF16)
    qs = []
    for h in range(n_heads):
        q = q_ref[:, h * LANES:(h + 1) * LANES]
        qs += [jnp.where(lane < HEAD_DIM, q, zero), jnp.where(lane >= HEAD_DIM, q, zero)]

    def tile(kt, carry, masked=False):
        start = pl.multiple_of(kt * tk, tk)
        k = [k_ref[pl.ds(start, tk), h * LANES:(h + 1) * LANES] for h in range(n_heads)]
        v = [_with_ones(v_ref[pl.ds(start, tk), h * LANES:(h + 1) * LANES]) for h in range(n_heads)]
        s = [_dot_nt(qs[c], k[c // 2]) for c in range(2 * n_heads)]
        if masked:
            ok = (kt * tk + key_in_tile) <= (qi * tq + row)
            s = [jnp.where(ok, x, NEG_INF) for x in s]
        return _online_steps(s, [v[c // 2] for c in range(2 * n_heads)], carry)

    n_full = (qi * tq) // tk
    carry = lax.fori_loop(0, n_full, tile, _online_init(tq, 2 * n_heads))
    for j in range(n_diag):
        carry = tile(n_full + j, carry, True)

    lam_rows = lam_ref[...]
    lam = (jnp.exp(jnp.sum(lam_rows[0:1] * lam_rows[1:2], axis=-1, keepdims=True))
           - jnp.exp(jnp.sum(lam_rows[2:3] * lam_rows[3:4], axis=-1, keepdims=True)) + lam_init)
    for h in range(n_heads):
        o = _online_result(carry[2 * h]) - lam * _online_result(carry[2 * h + 1])
        o_ref[:, h * LANES:(h + 1) * LANES] = (_rms(o, sub_ref[...]) * (1.0 - lam_init)).astype(o_ref.dtype)


def _diff(dq, dk, dv, lam_rows, subln, lam_init, batch, seq, tq, tk, n_heads=4):
    nq = seq // tq
    w = n_heads * LANES
    return pl.pallas_call(
        functools.partial(_diff_kernel, tq=tq, tk=tk, n_heads=n_heads, lam_init=lam_init),
        grid=(batch, DIFF_HEADS // n_heads, nq),
        in_specs=[pl.BlockSpec((tq, w), lambda b, h, q: (b * nq + q, h)),
                  pl.BlockSpec((seq, w), lambda b, h, q: (b, h)),
                  pl.BlockSpec((seq, w), lambda b, h, q: (b, h)),
                  _resident(lam_rows.shape), _resident(subln.shape)],
        out_specs=pl.BlockSpec((tq, w), lambda b, h, q: (b * nq + q, h)),
        out_shape=jax.ShapeDtypeStruct((batch * seq, DIFF_V_W), BF16),
        compiler_params=_params(("parallel", "parallel", "arbitrary")),
        name="diff_attention",
    )(dq, dk, dv, lam_rows, subln)


def _sb_kernel(q_ref, k_ref, v_ref, u_ref, o_ref, *, tq, tk, n_pairs):
    qi = pl.program_id(2)
    n_diag = tq // tk
    n_ch = 2 * n_pairs
    lane = lax.broadcasted_iota(jnp.int32, (1, LANES), 1)
    row = lax.broadcasted_iota(jnp.int32, (tq, 1), 0)
    key_in_tile = lax.broadcasted_iota(jnp.int32, (1, tk), 1)
    zero = jnp.zeros((), BF16)
    qa = []
    for p in range(n_pairs):
        q = q_ref[:, p * LANES:(p + 1) * LANES]
        qa += [jnp.where(lane < HEAD_DIM, q, zero), jnp.where(lane >= HEAD_DIM, q, zero)]
    u2 = u_ref[...]

    def rows(ref, kt, p):
        return ref[pl.ds(pl.multiple_of(kt * tk, tk), tk), p * LANES:(p + 1) * LANES]

    def sweep(kts, carry, stricts=None):
        c, acc = carry
        chains = [(i, h) for i in range(len(kts)) for h in range(n_ch)]
        strict = {i: None if stricts is None else stricts[i] for i in range(len(kts))}
        z = {}
        for i, kt in enumerate(kts):
            k = [rows(k_ref, kt, p) for p in range(n_pairs)]
            for h in range(n_ch):
                z[i, h] = _dot_nt(qa[h], k[h // 2])
        nlr, pieces = {}, {}
        for ch in chains:
            n = jnp.where(z[ch] > SOFTPLUS_LINEAR, z[ch], jnp.log(1.0 + jnp.exp(z[ch])))
            if strict[ch[0]] is not None:
                n = jnp.where(strict[ch[0]], n, 0.0)
            hi = n.astype(BF16)
            lo = (n - hi.astype(F32)).astype(BF16)
            nlr[ch] = n
            pieces[ch] = jnp.concatenate([hi, lo], axis=1)
        nsuf = {ch: _dot(pieces[ch], u2) for ch in chains}
        w = {}
        offset = list(c)
        for i in range(len(kts)):
            for h in range(n_ch):
                wa = jnp.exp(z[i, h] - nlr[i, h] - nsuf[i, h] - offset[h])
                if strict[i] is not None:
                    wa = jnp.where(strict[i], wa, 0.0)
                w[i, h] = wa.astype(BF16)
                offset[h] = offset[h] + nsuf[i, h][:, 0:1] + nlr[i, h][:, 0:1]
        v = {(i, p): rows(v_ref, kt, p) for i, kt in enumerate(kts) for p in range(n_pairs)}
        pv = {ch: _dot(w[ch], v[ch[0], ch[1] // 2]) for ch in chains}
        acc = tuple(acc[h] + sum(pv[i, h] for i in range(len(kts))) for h in range(n_ch))
        return tuple(offset), acc

    def smallest(c):
        return functools.reduce(jnp.minimum, [jnp.min(x) for x in c])

    carry = (tuple(jnp.zeros((tq, 1), F32) for _ in range(n_ch)),
             tuple(jnp.zeros((tq, LANES), F32) for _ in range(n_ch)))
    diag = list(reversed(range(n_diag)))
    carry = sweep([qi * n_diag + j for j in diag], carry, [(j * tk + key_in_tile) < row for j in diag])
    n_off = qi * n_diag

    def more(state):
        r, c_min, _ = state
        return jnp.logical_and(r < n_off, c_min < SB_UNDERFLOW)

    def step(state):
        r, _, cr = state
        cr = sweep([n_off - 1 - r], cr)
        return r + 1, smallest(cr[0]), cr

    _, _, (_, acc) = lax.while_loop(more, step, (jnp.int32(0), smallest(carry[0]), carry))
    for p in range(n_pairs):
        o_ref[:, p * LANES:(p + 1) * LANES] = jnp.where(
            lane < HEAD_DIM, acc[2 * p], acc[2 * p + 1]).astype(o_ref.dtype)


def _sb(qkv, u_mat, batch, seq, tq, tk, n_pairs):
    nq = seq // tq
    n_blocks = SB_HEADS // 2 // n_pairs
    w = n_pairs * LANES
    return pl.pallas_call(
        functools.partial(_sb_kernel, tq=tq, tk=tk, n_pairs=n_pairs),
        grid=(batch, n_blocks, nq),
        in_specs=[pl.BlockSpec((tq, w), lambda b, p, q: (b * nq + q, p)),
                  pl.BlockSpec((seq, w), lambda b, p, q: (b, n_blocks + p)),
                  pl.BlockSpec((seq, w), lambda b, p, q: (b, 2 * n_blocks + p)),
                  _resident(u_mat.shape)],
        out_specs=pl.BlockSpec((tq, w), lambda b, p, q: (b * nq + q, p)),
        out_shape=jax.ShapeDtypeStruct((batch * seq, D_MODEL), BF16),
        compiler_params=_params(("parallel", "parallel", "arbitrary")),
        name="sb_attention",
    )(qkv, qkv, qkv, u_mat)


def _mix_ffn_kernel(x_ref, oa_ref, ob_ref, wo_ref, g_ref, wg_ref, wu_ref, wd_ref, gf_ref, out_ref, *,
                    final_norm):
    half = oa_ref.shape[1]
    x = x_ref[...] + _dot(oa_ref[...], wo_ref[0:half, :]) + _dot(ob_ref[...], wo_ref[half:2 * half, :])
    h = _rms(x, g_ref[...]).astype(BF16)
    gate = _dot(h, wg_ref[...])
    up = _dot(h, wu_ref[...])
    act = (gate * jax.nn.sigmoid(gate) * up).astype(BF16)
    acc = x + _dot(act, wd_ref[...])
    if final_norm:
        acc = _rms(acc, gf_ref[...])
    out_ref[...] = acc


def _mix_ffn(x2, oa, ob, oa_blk, ob_blk, wo, g, wg, wu, wd, gf, final_norm, tm=512):
    T = x2.shape[0]
    half = D_MODEL // 2
    return pl.pallas_call(
        functools.partial(_mix_ffn_kernel, final_norm=final_norm),
        grid=(T // tm,),
        in_specs=[pl.BlockSpec((tm, D_MODEL), lambda m: (m, 0)),
                  pl.BlockSpec((tm, half), lambda m: (m, oa_blk)),
                  pl.BlockSpec((tm, half), lambda m: (m, ob_blk)),
                  _resident(wo.shape), _resident((1, D_MODEL)),
                  _resident(wg.shape), _resident(wu.shape), _resident(wd.shape), _resident((1, D_MODEL))],
        out_specs=pl.BlockSpec((tm, D_MODEL), lambda m: (m, 0)),
        out_shape=jax.ShapeDtypeStruct((T, D_MODEL), F32),
        compiler_params=_params(("parallel",)),
        name="outproj_ffn",
    )(x2, oa, ob, wo, g.reshape(1, D_MODEL), wg, wu, wd, gf.reshape(1, D_MODEL))


def _cast_kernel(w_ref, o_ref):
    o_ref[...] = w_ref[...].astype(o_ref.dtype)


def _to_bf16(w, layer, rows=256):
    _, n, c = w.shape
    return pl.pallas_call(
        _cast_kernel,
        grid=(n // rows,),
        in_specs=[pl.BlockSpec((None, rows, c), lambda i: (layer, i, 0))],
        out_specs=pl.BlockSpec((rows, c), lambda i: (i, 0)),
        out_shape=jax.ShapeDtypeStruct((n, c), BF16),
        compiler_params=_params(("parallel",)),
        name="cast_bf16",
    )(w)


def _even_in_columns():
    off = np.cumsum([0, NSA_Q_W, NSA_KV_W, NSA_KV_W, NSA_KV_W, NSA_KV_W, NSA_KV_W, NSA_KV_W,
                     NSA_GATE_W, DIFF_QK_W, DIFF_QK_W, DIFF_V_W])
    nq, kc, vc, ks, vs, kw, vw, gl, dq, dk, dv = (int(o) for o in off[:-1])
    q_ranges = [(nq + HEAD_DIM * (g * NSA_HPG + i), HEAD_DIM) for i in range(NSA_HPG) for g in range(NSA_GROUPS)]
    return [(q_ranges, True, True), ([(kc, NSA_KV_W)], True, False), ([(ks, NSA_KV_W)], True, False),
            ([(kw, NSA_KV_W)], True, False), ([(dq, DIFF_QK_W)], True, True), ([(dk, DIFF_QK_W)], True, False),
            ([(vc, NSA_KV_W)], False, False), ([(vs, NSA_KV_W)], False, False), ([(vw, NSA_KV_W)], False, False),
            ([(dv, DIFF_V_W)], False, False), ([(gl, NSA_GATE_W)], False, False)]


def _prep_even_w_in(w):
    pieces = _even_in_columns()
    scale = HEAD_DIM ** -0.5
    cols, widths = [], []
    for ranges, _, scaled in pieces:
        width = 0
        for start, wd in ranges:
            blk = w[:, start:start + wd]
            cols.append(blk * scale if scaled else blk)
            width += wd
        pad = (-width) % LANES
        if pad:
            cols.append(jnp.zeros((w.shape[0], pad), w.dtype))
        widths.append(width + pad)
    n_rope = sum(1 for _, rope, _ in pieces if rope)
    return jnp.concatenate(cols, axis=1).astype(BF16), widths, n_rope


def _prep_even_w_out(w):
    rows = [w[(g * NSA_HPG + i) * HEAD_DIM:(g * NSA_HPG + i + 1) * HEAD_DIM]
            for i in range(NSA_HPG) for g in range(NSA_GROUPS)]
    return jnp.concatenate(rows + [w[NSA_Q_W:]], axis=0).astype(BF16)


def _prep_cmp(pos, w):
    half = CMP_BLOCK // 2
    p2 = jnp.tile(pos.reshape(2, half, 1, HEAD_DIM), (1, 1, NSA_GROUPS, 1)).reshape(2, half * LANES)
    w4 = w.reshape(2, half, HEAD_DIM, HEAD_DIM).astype(BF16)
    zeros = jnp.zeros_like(w4)
    wbd = jnp.stack([jnp.concatenate([w4 if c == g else zeros for c in range(NSA_GROUPS)], axis=-1)
                     for g in range(NSA_GROUPS)], axis=2)
    return p2.astype(F32), wbd.reshape(2, half * LANES, LANES)


def _rope_tables(seq):
    inv = 1.0 / (ROPE_THETA ** (jnp.arange(0, HEAD_DIM, 2, dtype=F32) / HEAD_DIM))
    ang = jnp.arange(seq, dtype=F32)[:, None] * inv[None, :]
    cos, sin = jnp.cos(ang), jnp.sin(ang)
    reps = LANES // HEAD_DIM
    cos_t = jnp.tile(jnp.concatenate([cos, cos], axis=1), (1, reps))
    sin_t = jnp.tile(jnp.concatenate([-sin, sin], axis=1), (1, reps))
    return cos_t, sin_t


def _selection_constants(seq, sb_tk):
    n_cmp_rows = seq // CMP_STRIDE
    n_slc = seq // SEL_BLOCK
    n = np.arange(n_cmp_rows)[None, :] * CMP_STRIDE
    j = np.arange(n_slc)[:, None] * SEL_BLOCK
    ct = ((n < j + SEL_BLOCK) & (n + CMP_BLOCK > j)).astype(np.float32)
    e = np.zeros((seq, LANES), np.float32)
    e[np.arange(seq), np.arange(seq) // SEL_BLOCK] = NEG_INF
    u =(np.arange(sb_tk)[:, None] > np.arange(sb_tk)[None, :]).astype(np.float32)
    return jnp.asarray(ct, BF16), jnp.asarray(e, BF16), jnp.asarray(np.concatenate([u, u], axis=0), BF16)


def kernel(x, norm_mix, norm_ffn, norm_final, even_w_in, even_w_out, cmp_pos_k, cmp_w_k, cmp_pos_v, cmp_w_v,
           diff_lq1, diff_lk1, diff_lq2, diff_lk2, diff_subln, odd_w_in, odd_w_out, ffn_w_gate, ffn_w_up,
           ffn_w_down):
    batch, seq, _ = x.shape
    nsa_tq, nsa_tk = 256, 1024
    sb_tq, sb_tk, sb_pairs = 256, 256, 4
    diff_tq, diff_tk = 256, 1024
    assert seq % sb_tq == 0 and seq % diff_tk == 0 and seq // SEL_BLOCK <= LANES and seq // SEL_BLOCK >= SEL_TOPK
    cos_t, sin_t = _rope_tables(seq)
    ct, e_mat, u_mat = _selection_constants(seq, sb_tk)
    sb_scale = jnp.concatenate([jnp.full((D_MODEL,), HEAD_DIM ** -0.5, F32), jnp.ones((2 * D_MODEL,), F32)])

    x2 = x.reshape(batch * seq, D_MODEL)
    for layer in range(DEPTH):
        last = layer == DEPTH - 1
        if layer % 2 == 0:
            e = layer // 2
            lam_init = 0.8 - 0.6 * math.exp(-0.3 * layer)
            w_in, widths, n_rope = _prep_even_w_in(even_w_in[e])
            qn, kc, ks, kw, dq, dk, vc, vs, vw, dv, gl = _inproj(
                x2, norm_mix[layer], w_in, cos_t, sin_t, widths, n_rope, True, seq)
            pk, wk = _prep_cmp(cmp_pos_k[e], cmp_w_k[e])
            pv, wv = _prep_cmp(cmp_pos_v[e], cmp_w_v[e])
            kcmp, vcmp = _compress(kc, vc, pk, pv, wk, wv, batch, seq)
            o_nsa = _nsa(qn, kcmp, vcmp, ks, vs, kw, vw, gl, ct, e_mat, batch, seq, nsa_tq, nsa_tk)
            lam_rows = jnp.zeros((8, LANES), F32).at[0:4, 0:HEAD_DIM].set(
                jnp.stack([diff_lq1[e], diff_lk1[e], diff_lq2[e], diff_lk2[e]]))
            o_diff = _diff(dq, dk, dv, lam_rows, diff_subln[e].reshape(1, DIFF_VDIM), lam_init, batch, seq,
                           diff_tq, diff_tk)
            wo = _prep_even_w_out(even_w_out[e])
            oa, ob, oa_blk, ob_blk = o_nsa, o_diff, 0, 0
        else:
            o = layer // 2
            w_in = (odd_w_in[o] * sb_scale[None, :]).astype(BF16)
            (qkv,) = _inproj(x2, norm_mix[layer], w_in, cos_t, sin_t, [3 * D_MODEL], 0, False, seq)
            o_sb = _sb(qkv, u_mat, batch, seq, sb_tq, sb_tk, sb_pairs)
            wo = odd_w_out[o].astype(BF16)
            oa, ob, oa_blk, ob_blk = o_sb, o_sb, 0, 1
        x2 = _mix_ffn(x2, oa, ob, oa_blk, ob_blk, wo, norm_ffn[layer],
                      _to_bf16(ffn_w_gate, layer), _to_bf16(ffn_w_up, layer),
                      _to_bf16(ffn_w_down, layer), norm_final, last)
    return x2.reshape(batch, seq, D_MODEL)
```

```python
import functools
import math

import numpy as np
import jax
import jax.numpy as jnp
from jax import lax
from jax.experimental import pallas as pl
from jax.experimental.pallas import tpu as pltpu

D_MODEL = 1024
DEPTH = 4
HEAD_DIM = 64
ROPE_THETA = 10000.0
NORM_EPS = 1e-6
NEG_INF = -1e30
BIG = 1e30
SOFTPLUS_LINEAR = 40.0
SB_UNDERFLOW = 110.0

NSA_HEADS = 8
NSA_GROUPS = 2
NSA_HPG = NSA_HEADS // NSA_GROUPS
CMP_BLOCK = 32
CMP_STRIDE = 16
SEL_BLOCK = 64
SEL_TOPK = 16
WINDOW = 512

DIFF_HEADS = 4
DIFF_VDIM = 2 * HEAD_DIM

SB_HEADS = D_MODEL // HEAD_DIM
FFN_HIDDEN = -(-(8 * D_MODEL) // (3 * 256)) * 256

NSA_Q_W = NSA_HEADS * HEAD_DIM
NSA_KV_W = NSA_GROUPS * HEAD_DIM
NSA_GATE_W = 3 * NSA_HEADS
DIFF_QK_W = DIFF_HEADS * 2 * HEAD_DIM
DIFF_V_W = DIFF_HEADS * DIFF_VDIM

LANES = 128
SUBLANES = 8
BF16_SUBLANES = 16
VMEM_LIMIT = 56 * 1024 * 1024

BF16 = jnp.bfloat16
F32 = jnp.float32


def _dot(a, b):
    return jnp.dot(a, b, preferred_element_type=F32)


def _dot_nt(a, b):
    return lax.dot_general(a, b, (((1,), (1,)), ((), ())), preferred_element_type=F32)


def _rms(x, g):
    return x * lax.rsqrt(jnp.mean(x * x, axis=-1, keepdims=True) + NORM_EPS) * g


def _params(sem, vmem=VMEM_LIMIT):
    return pltpu.CompilerParams(dimension_semantics=sem, vmem_limit_bytes=vmem)


def _resident(shape):
    nd = len(shape)
    return pl.BlockSpec(shape, lambda *_: (0,) * nd, pipeline_mode=pl.Buffered(1))


def _inproj_kernel(x_ref, g_ref, w_ref, cos_ref, sin_ref, *out_refs, widths, n_rope, gate_last):
    h = _rms(x_ref[...], g_ref[...]).astype(BF16)
    lane = lax.broadcasted_iota(jnp.int32, (1, LANES), 1)
    first_half = (lane % HEAD_DIM) < (HEAD_DIM // 2)
    rope_w = sum(widths[:n_rope])
    total_w = sum(widths)
    y_rope = _dot(h, w_ref[:, 0:rope_w]) if n_rope else None
    y_rest = _dot(h, w_ref[:, rope_w:total_w])
    col = 0
    for oi, (o_ref, w) in enumerate(zip(out_refs, widths)):
        if oi < n_rope:
            cos, sin = cos_ref[...], sin_ref[...]
            for b in range(w // LANES):
                yb = y_rope[:, col + b * LANES:col + (b + 1) * LANES]
                partner = jnp.where(first_half, pltpu.roll(yb, LANES - HEAD_DIM // 2, 1),
                                    pltpu.roll(yb, HEAD_DIM // 2, 1))
                o_ref[:, b * LANES:(b + 1) * LANES] = (yb * cos + partner * sin).astype(o_ref.dtype)
        else:
            o_ref[...] = y_rest[:, col - rope_w:col - rope_w + w].astype(o_ref.dtype)
        col += w


def _inproj(x2, g, w, cos_t, sin_t, widths, n_rope, gate_last, seq, tm=512):
    T = x2.shape[0]
    n_pos_blocks = seq // tm
    out_shape = [jax.ShapeDtypeStruct((T, wd), F32 if (gate_last and i == len(widths) - 1) else BF16)
                 for i, wd in enumerate(widths)]
    out_specs = [pl.BlockSpec((tm, wd), lambda m: (m, 0)) for wd in widths]
    return pl.pallas_call(
        functools.partial(_inproj_kernel, widths=tuple(widths), n_rope=n_rope, gate_last=gate_last),
        grid=(T // tm,),
        in_specs=[pl.BlockSpec((tm, D_MODEL), lambda m: (m, 0)),
                  _resident((1, D_MODEL)),
                  _resident(w.shape),
                  pl.BlockSpec((tm, LANES), lambda m: (m % n_pos_blocks, 0)),
                  pl.BlockSpec((tm, LANES), lambda m: (m % n_pos_blocks, 0))],
        out_specs=out_specs,
        out_shape=out_shape,
        compiler_params=_params(("parallel",)),
        name="inproj",
    )(x2, g.reshape(1, D_MODEL), w, cos_t, sin_t)


def _compress_kernel(kc_ref, vc_ref, pk_ref, pv_ref, wk_ref, wv_ref, ko_ref, vo_ref):
    def one(x_ref, p_ref, w_ref, o_ref):
        x = x_ref[...].astype(F32)
        first = _dot((x + p_ref[0:1, :]).astype(BF16), w_ref[0])
        second = _dot((x + p_ref[1:2, :]).astype(BF16), w_ref[1])
        n = second.shape[0]
        o_ref[...] = (first + pltpu.roll(second, n - 1, 0)).astype(o_ref.dtype)

    one(kc_ref, pk_ref, wk_ref, ko_ref)
    one(vc_ref, pv_ref, wv_ref, vo_ref)


def _compress(kc, vc, pk, pv, wk, wv, batch, seq):
    rows = seq // CMP_STRIDE
    width = CMP_STRIDE * LANES
    kc3 = kc.reshape(batch, rows, width)
    vc3 = vc.reshape(batch, rows, width)
    blk = pl.BlockSpec((None, rows, width), lambda b: (b, 0, 0))
    oblk = pl.BlockSpec((None, rows, LANES), lambda b: (b, 0, 0))
    return pl.pallas_call(
        _compress_kernel,
        grid=(batch,),
        in_specs=[blk, blk, _resident(pk.shape), _resident(pv.shape), _resident(wk.shape), _resident(wv.shape)],
        out_specs=[oblk, oblk],
        out_shape=[jax.ShapeDtypeStruct((batch, rows, LANES), BF16)] * 2,
        compiler_params=_params(("parallel",)),
        name="nsa_compress",
    )(kc3, vc3, pk, pv, wk, wv)


def _with_ones(v):
    return jnp.concatenate([v, jnp.ones_like(v)], axis=1)


def _online_steps(scores, values, carries):
    m_new = [jnp.maximum(m, jnp.max(s, axis=-1, keepdims=True)) for s, (m, _) in zip(scores, carries)]
    p = [jnp.exp(s - mn) for s, mn in zip(scores, m_new)]
    pv = [_dot(pi.astype(BF16), v) for pi, v in zip(p, values)]
    return [(mn, jnp.exp(m - mn) * acc + pvi) for pvi, mn, (m, acc) in zip(pv, m_new, carries)]


def _online_init(tq, n):
    return [(jnp.full((tq, 1), NEG_INF, F32), jnp.zeros((tq, 2 * LANES), F32)) for _ in range(n)]


def _online_result(carry):
    _, acc = carry
    return acc[:, :LANES] / acc[:, LANES:]


def _nsa_kernel(q_ref, kcmp_ref, vcmp_ref, ks_ref, vs_ref, kw_ref, vw_ref, gl_ref, ct_ref, e_ref,
                o_ref, *, tq, tk, n_cmp_rows, n_slc):
    qi = pl.program_id(1)
    q0 = qi * tq
    lane = lax.broadcasted_iota(jnp.int32, (1, LANES), 1)
    half = [lane < HEAD_DIM, lane >= HEAD_DIM]
    t_col = q0 + lax.broadcasted_iota(jnp.int32, (tq, 1), 0)

    qm = [[jnp.where(half[g], q_ref[:, i * LANES:(i + 1) * LANES], jnp.zeros((), BF16))
           for i in range(NSA_HPG)] for g in range(NSA_GROUPS)]

    heads = [(g, i) for g in range(NSA_GROUPS) for i in range(NSA_HPG)]
    n_heads = len(heads)

    n_idx = lax.broadcasted_iota(jnp.int32, (1, n_cmp_rows), 1)
    cmp_ok = (n_idx * CMP_STRIDE + (CMP_BLOCK - 1)) <= t_col
    kcmp = kcmp_ref[...]
    vcmp = vcmp_ref[...]
    s_cmp = [_dot_nt(qm[g][i], kcmp) for g, i in heads]
    p_cmp = []
    for s in s_cmp:
        s = jnp.where(cmp_ok, s, NEG_INF)
        e = jnp.where(cmp_ok, jnp.exp(s - jnp.max(s, axis=-1, keepdims=True)), 0.0)
        l = jnp.sum(e, axis=-1, keepdims=True)
        p_cmp.append(e * jnp.where(l > 0.0, 1.0 / l, 0.0))
    o_cmp = [_dot(p.astype(BF16), vcmp) for p in p_cmp]
    p_sum = [sum(p_cmp[g * NSA_HPG + i] for i in range(NSA_HPG)) for g in range(NSA_GROUPS)]

    n_grp = n_slc // SUBLANES
    jb = lax.broadcasted_iota(jnp.int32, (n_slc, 1), 0)
    sub = lax.broadcasted_iota(jnp.int32, (SUBLANES, 1), 0)
    qblk = (q0 + lax.broadcasted_iota(jnp.int32, (1, tq), 1)) // SEL_BLOCK
    forced = (jb == 0) | (jb == qblk) | (jb == qblk - 1)
    future = jb > qblk
    key_in_tile = lax.broadcasted_iota(jnp.int32, (1, tk), 1)
    ct = ct_ref[...]
    dropped = []
    for g in range(NSA_GROUPS):
        r0 = p_sum[g]
        p_hi = r0.astype(BF16)
        r1 = r0 - p_hi.astype(F32)
        p_mid = r1.astype(BF16)
        p_lo = (r1 - p_mid.astype(F32)).astype(BF16)
        imp = _dot_nt(ct, p_hi) + _dot_nt(ct, p_mid) + _dot_nt(ct, p_lo)
        imp = jnp.where(forced, BIG, jnp.where(future, -BIG, imp))
        imp_g = [imp[r * SUBLANES:(r + 1) * SUBLANES, :] for r in range(n_grp)]
        cnt = [jnp.zeros((SUBLANES, tq), F32) for _ in range(n_grp)]
        for j in range(n_slc):
            r, jj = divmod(j, SUBLANES)
            row = imp_g[r][jj:jj + 1, :]
            for b in range(n_grp):
                if b < r:
                    inc = jnp.where(row > imp_g[b], 1.0, 0.0)
                elif b > r:
                    inc = jnp.where(row >= imp_g[b], 1.0, 0.0)
                else:
                    tie = jnp.where(sub > jj, 1.0, 0.0)
                    inc = jnp.where(row > imp_g[b], 1.0, jnp.where(row == imp_g[b], tie, 0.0))
                cnt[b] = cnt[b] + inc
        drop_t = [jnp.where((cnt[b] < float(SEL_TOPK)) & (imp_g[b] >= 0.0), 0.0, 1.0) for b in range(n_grp)]
        drop_t = jnp.concatenate(drop_t + [jnp.ones((LANES - n_slc, tq), F32)], axis=0)
        dropped.append(drop_t.T.astype(BF16))

    q_sel = [jnp.concatenate([qm[g][i], dropped[g]], axis=1) for g, i in heads]

    def sel_tile(kt, carry, masked=False):
        start = pl.multiple_of(kt * tk, tk)
        k = jnp.concatenate([ks_ref[pl.ds(start, tk), :], e_ref[pl.ds(start, tk), :]], axis=1)
        v = _with_ones(vs_ref[pl.ds(start, tk), :])
        s = [_dot_nt(q, k) for q in q_sel]
        if masked:
            ok = (kt * tk + key_in_tile) <= t_col
            s = [jnp.where(ok, x, NEG_INF) for x in s]
        return _online_steps(s, [v] * n_heads, carry)

    n_kt = (q0 + tq + tk - 1) // tk
    carry = lax.fori_loop(0, n_kt - 1, sel_tile, _online_init(tq, n_heads))
    o_sel = [_online_result(c) for c in sel_tile(n_kt - 1, carry, True)]

    span = WINDOW + tq
    start = pl.multiple_of(jnp.maximum(q0 - WINDOW, 0), tq)
    key = start + lax.broadcasted_iota(jnp.int32, (1, span), 1)
    ok = (key <= t_col) & (key > t_col - WINDOW)
    k = kw_ref[pl.ds(start, span), :]
    v = _with_ones(vw_ref[pl.ds(start, span), :])
    o_win = []
    for g in range(NSA_GROUPS):
        s_win = [jnp.where(ok, _dot_nt(qm[g][i], k), NEG_INF) for i in range(NSA_HPG)]
        p_win = [jnp.exp(s - jnp.max(s, axis=-1, keepdims=True)) for s in s_win]
        pv_win = [_dot(p.astype(BF16), v) for p in p_win]
        o_win += [pv[:, :LANES] / pv[:, LANES:] for pv in pv_win]

    gates = jax.nn.sigmoid(gl_ref[...])
    for i in range(NSA_HPG):
        outs = []
        for g in range(NSA_GROUPS):
            hd = g * NSA_HPG + i
            outs.append(gates[:, 3 * hd:3 * hd + 1] * o_cmp[hd]
                        + gates[:, 3 * hd + 1:3 * hd + 2] * o_sel[hd]
                        + gates[:, 3 * hd + 2:3 * hd + 3] * o_win[hd])
        o_ref[:, i * LANES:(i + 1) * LANES] = jnp.where(half[0], outs[0], outs[1]).astype(o_ref.dtype)


def _nsa(qn, kcmp, vcmp, ks, vs, kw, vw, gl, ct, e_mat, batch, seq, tq, tk):
    nq = seq // tq
    n_cmp_rows = seq // CMP_STRIDE
    n_slc = seq // SEL_BLOCK
    row_blk = lambda w: pl.BlockSpec((tq, w), lambda b, q: (b * nq + q, 0))
    seq_blk = pl.BlockSpec((seq, LANES), lambda b, q: (b, 0))
    cmp_blk = pl.BlockSpec((None, n_cmp_rows, LANES), lambda b, q: (b, 0, 0))
    return pl.pallas_call(
        functools.partial(_nsa_kernel, tq=tq, tk=tk, n_cmp_rows=n_cmp_rows, n_slc=n_slc),
        grid=(batch, nq),
        in_specs=[row_blk(NSA_Q_W), cmp_blk, cmp_blk, seq_blk, seq_blk, seq_blk, seq_blk, row_blk(LANES),
                  _resident(ct.shape), _resident(e_mat.shape)],
        out_specs=row_blk(NSA_Q_W),
        out_shape=jax.ShapeDtypeStruct((batch * seq, NSA_Q_W), BF16),
        compiler_params=_params(("parallel", "arbitrary")),
        name="nsa_attention",
    )(qn, kcmp, vcmp, ks, vs, kw, vw, gl, ct, e_mat)


def _diff_kernel(q_ref, k_ref, vt_ref, lam_ref, sub_ref, o_ref, *, tq, tk, n_heads, lam_init):
    qi = pl.program_id(2)
    n_diag = max(tq // tk, 1)
    lane = lax.broadcasted_iota(jnp.int32, (1, LANES), 1)
    q_pos = qi * tq + lax.broadcasted_iota(jnp.int32, (1, tq), 1)
    key_in_tile = lax.broadcasted_iota(jnp.int32, (tk, 1), 0)
    zero = jnp.zeros((), BF16)
    qs = []
    for h in range(n_heads):
        q = q_ref[:, h * LANES:(h + 1) * LANES]
        qs += [jnp.where(lane < HEAD_DIM, q, zero), jnp.where(lane >= HEAD_DIM, q, zero)]
    n_ch = 2 * n_heads
    ones = jnp.ones((BF16_SUBLANES, tk), BF16)

    def tile(kt, carry, masked=False):
        start = pl.multiple_of(kt * tk, tk)
        k = [k_ref[pl.ds(start, tk), h * LANES:(h + 1) * LANES] for h in range(n_heads)]
        vt = [jnp.concatenate([vt_ref[kt, h * LANES:(h + 1) * LANES, :], ones], axis=0) for h in range(n_heads)]
        s = [_dot_nt(k[c // 2], qs[c]) for c in range(n_ch)]
        if masked:
            ok = (kt * tk + key_in_tile) <= q_pos
            s = [jnp.where(ok, x, NEG_INF) for x in s]
        m_new = [jnp.maximum(m, jnp.max(x, axis=0, keepdims=True)) for x, (m, _) in zip(s, carry)]
        p = [jnp.exp(x - mn) for x, mn in zip(s, m_new)]
        pv = [_dot(vt[c // 2], p[c].astype(BF16)) for c in range(n_ch)]
        return [(mn, jnp.exp(m - mn) * acc + pvc) for pvc, mn, (m, acc) in zip(pv, m_new, carry)]

    init = [(jnp.full((1, tq), NEG_INF, F32), jnp.zeros((LANES + BF16_SUBLANES, tq), F32)) for _ in range(n_ch)]
    n_full = (qi * tq) // tk
    carry = lax.fori_loop(0, n_full, tile, init)
    for j in range(n_diag):
        carry = tile(n_full + j, carry, True)

    lam_rows = lam_ref[...]
    lam = (jnp.exp(jnp.sum(lam_rows[0:1] * lam_rows[1:2], axis=-1, keepdims=True))
           - jnp.exp(jnp.sum(lam_rows[2:3] * lam_rows[3:4], axis=-1, keepdims=True)) + lam_init)
    result = lambda acc: acc[:LANES] / acc[LANES:LANES + 1]
    for h in range(n_heads):
        o = result(carry[2 * h][1]) - lam * result(carry[2 * h + 1][1])
        y = o * lax.rsqrt(jnp.mean(o * o, axis=0, keepdims=True) + NORM_EPS) * sub_ref[...]
        o_ref[h * LANES:(h + 1) * LANES, :] = (y * (1.0 - lam_init)).astype(o_ref.dtype)


def _diff(dq, dk, dv, lam_rows, subln, lam_init, batch, seq, tq, tk, n_heads=4):
    nq = seq // tq
    n_kt = seq // tk
    w = n_heads * LANES
    n_blk = DIFF_HEADS // n_heads
    vt = dv.reshape(batch, n_kt, tk, DIFF_V_W).transpose(0, 1, 3, 2)
    out_t = pl.pallas_call(
        functools.partial(_diff_kernel, tq=tq, tk=tk, n_heads=n_heads, lam_init=lam_init),
        grid=(batch, n_blk, nq),
        in_specs=[pl.BlockSpec((tq, w), lambda b, h, q: (b * nq + q, h)),
                  pl.BlockSpec((seq, w), lambda b, h, q: (b, h)),
                  pl.BlockSpec((None, n_kt, w, tk), lambda b, h, q: (b, 0, h, 0)),
                  _resident(lam_rows.shape), _resident(subln.shape)],
        out_specs=pl.BlockSpec((None, w, tq), lambda b, h, q: (b, h, q)),
        out_shape=jax.ShapeDtypeStruct((batch, DIFF_V_W, seq), BF16),
        compiler_params=_params(("parallel", "parallel", "arbitrary")),
        name="diff_attention",
    )(dq, dk, vt, lam_rows, subln)
    return out_t.transpose(0, 2, 1).reshape(batch * seq, DIFF_V_W)


def _sb_kernel(q_ref, k_ref, v_ref, u_ref, o_ref, *, tq, tk, n_pairs):
    qi = pl.program_id(2)
    n_diag = tq // tk
    n_ch = 2 * n_pairs
    lane = lax.broadcasted_iota(jnp.int32, (1, LANES), 1)
    row = lax.broadcasted_iota(jnp.int32, (tq, 1), 0)
    key_in_tile = lax.broadcasted_iota(jnp.int32, (1, tk), 1)
    zero = jnp.zeros((), BF16)
    qa = []
    for p in range(n_pairs):
        q = q_ref[:, p * LANES:(p + 1) * LANES]
        qa += [jnp.where(lane < HEAD_DIM, q, zero), jnp.where(lane >= HEAD_DIM, q, zero)]
    u2 = u_ref[...]

    def rows(ref, kt, p):
        return ref[pl.ds(pl.multiple_of(kt * tk, tk), tk), p * LANES:(p + 1) * LANES]

    def sweep(kts, carry, stricts=None):
        c, acc = carry
        chains = [(i, h) for i in range(len(kts)) for h in range(n_ch)]
        strict = {i: None if stricts is None else stricts[i] for i in range(len(kts))}
        z = {}
        for i, kt in enumerate(kts):
            k = [rows(k_ref, kt, p) for p in range(n_pairs)]
            for h in range(n_ch):
                z[i, h] = _dot_nt(qa[h], k[h // 2])
        nlr, pieces = {}, {}
        for ch in chains:
            n = jnp.where(z[ch] > SOFTPLUS_LINEAR, z[ch], jnp.log(1.0 + jnp.exp(z[ch])))
            if strict[ch[0]] is not None:
                n = jnp.where(strict[ch[0]], n, 0.0)
            hi = n.astype(BF16)
            lo = (n - hi.astype(F32)).astype(BF16)
            nlr[ch] = n
            pieces[ch] = jnp.concatenate([hi, lo], axis=1)
        nsuf = {ch: _dot(pieces[ch], u2) for ch in chains}
        w = {}
        offset = list(c)
        for i in range(len(kts)):
            for h in range(n_ch):
                wa = jnp.exp(z[i, h] - nlr[i, h] - nsuf[i, h] - offset[h])
                if strict[i] is not None:
                    wa = jnp.where(strict[i], wa, 0.0)
                w[i, h] = wa.astype(BF16)
                offset[h] = offset[h] + nsuf[i, h][:, 0:1] + nlr[i, h][:, 0:1]
        v = {(i, p): rows(v_ref, kt, p) for i, kt in enumerate(kts) for p in range(n_pairs)}
        pv = {ch: _dot(w[ch], v[ch[0], ch[1] // 2]) for ch in chains}
        acc = tuple(acc[h] + sum(pv[i, h] for i in range(len(kts))) for h in range(n_ch))
        return tuple(offset), acc

    def smallest(c):
        return functools.reduce(jnp.minimum, [jnp.min(x) for x in c])

    carry = (tuple(jnp.zeros((tq, 1), F32) for _ in range(n_ch)),
             tuple(jnp.zeros((tq, LANES), F32) for _ in range(n_ch)))
    diag = list(reversed(range(n_diag)))
    carry = sweep([qi * n_diag + j for j in diag], carry, [(j * tk + key_in_tile) < row for j in diag])
    n_off = qi * n_diag

    def more(state):
        r, c_min, _ = state
        return jnp.logical_and(r < n_off, c_min < SB_UNDERFLOW)

    def step(state):
        r, _, cr = state
        cr = sweep([n_off - 1 - r], cr)
        return r + 1, smallest(cr[0]), cr

    _, _, (_, acc) = lax.while_loop(more, step, (jnp.int32(0), smallest(carry[0]), carry))
    for p in range(n_pairs):
        o_ref[:, p * LANES:(p + 1) * LANES] = jnp.where(
            lane < HEAD_DIM, acc[2 * p], acc[2 * p + 1]).astype(o_ref.dtype)


def _sb(qkv, u_mat, batch, seq, tq, tk, n_pairs):
    nq = seq // tq
    n_blocks = SB_HEADS // 2 // n_pairs
    w = n_pairs * LANES
    return pl.pallas_call(
        functools.partial(_sb_kernel, tq=tq, tk=tk, n_pairs=n_pairs),
        grid=(batch, n_blocks, nq),
        in_specs=[pl.BlockSpec((tq, w), lambda b, p, q: (b * nq + q, p)),
                  pl.BlockSpec((seq, w), lambda b, p, q: (b, n_blocks + p)),
                  pl.BlockSpec((seq, w), lambda b, p, q: (b, 2 * n_blocks + p)),
                  _resident(u_mat.shape)],
        out_specs=pl.BlockSpec((tq, w), lambda b, p, q: (b * nq + q, p)),
        out_shape=jax.ShapeDtypeStruct((batch * seq, D_MODEL), BF16),
        compiler_params=_params(("parallel", "parallel", "arbitrary")),
        name="sb_attention",
    )(qkv, qkv, qkv, u_mat)


def _mix_ffn_kernel(x_ref, oa_ref, ob_ref, wo_ref, g_ref, wg_ref, wu_ref, wd_ref, gf_ref, out_ref, *,
                    final_norm):
    half = oa_ref.shape[1]
    x = x_ref[...] + _dot(oa_ref[...], wo_ref[0:half, :]) + _dot(ob_ref[...], wo_ref[half:2 * half, :])
    h = _rms(x, g_ref[...]).astype(BF16)
    gate = _dot(h, wg_ref[...])
    up = _dot(h, wu_ref[...])
    act = (gate * jax.nn.sigmoid(gate) * up).astype(BF16)
    acc = x + _dot(act, wd_ref[...])
    if final_norm:
        acc = _rms(acc, gf_ref[...])
    out_ref[...] = acc


def _mix_ffn(x2, oa, ob, oa_blk, ob_blk, wo, g, wg, wu, wd, gf, final_norm, tm=512):
    T = x2.shape[0]
    half = D_MODEL // 2
    return pl.pallas_call(
        functools.partial(_mix_ffn_kernel, final_norm=final_norm),
        grid=(T // tm,),
        in_specs=[pl.BlockSpec((tm, D_MODEL), lambda m: (m, 0)),
                  pl.BlockSpec((tm, half), lambda m: (m, oa_blk)),
                  pl.BlockSpec((tm, half), lambda m: (m, ob_blk)),
                  _resident(wo.shape), _resident((1, D_MODEL)),
                  _resident(wg.shape), _resident(wu.shape), _resident(wd.shape), _resident((1, D_MODEL))],
        out_specs=pl.BlockSpec((tm, D_MODEL), lambda m: (m, 0)),
        out_shape=jax.ShapeDtypeStruct((T, D_MODEL), F32),
        compiler_params=_params(("parallel",)),
        name="outproj_ffn",
    )(x2, oa, ob, wo, g.reshape(1, D_MODEL), wg, wu, wd, gf.reshape(1, D_MODEL))


def _cast_kernel(w_ref, o_ref):
    o_ref[...] = w_ref[...].astype(o_ref.dtype)


def _to_bf16(w, layer, rows=256):
    _, n, c = w.shape
    return pl.pallas_call(
        _cast_kernel,
        grid=(n // rows,),
        in_specs=[pl.BlockSpec((None, rows, c), lambda i: (layer, i, 0))],
        out_specs=pl.BlockSpec((rows, c), lambda i: (i, 0)),
        out_shape=jax.ShapeDtypeStruct((n, c), BF16),
        compiler_params=_params(("parallel",)),
        name="cast_bf16",
    )(w)


def _even_in_columns():
    off = np.cumsum([0, NSA_Q_W, NSA_KV_W, NSA_KV_W, NSA_KV_W, NSA_KV_W, NSA_KV_W, NSA_KV_W,
                     NSA_GATE_W, DIFF_QK_W, DIFF_QK_W, DIFF_V_W])
    nq, kc, vc, ks, vs, kw, vw, gl, dq, dk, dv = (int(o) for o in off[:-1])
    q_ranges = [(nq + HEAD_DIM * (g * NSA_HPG + i), HEAD_DIM) for i in range(NSA_HPG) for g in range(NSA_GROUPS)]
    return [(q_ranges, True, True), ([(kc, NSA_KV_W)], True, False), ([(ks, NSA_KV_W)], True, False),
            ([(kw, NSA_KV_W)], True, False), ([(dq, DIFF_QK_W)], True, True), ([(dk, DIFF_QK_W)], True, False),
            ([(vc, NSA_KV_W)], False, False), ([(vs, NSA_KV_W)], False, False), ([(vw, NSA_KV_W)], False, False),
            ([(dv, DIFF_V_W)], False, False), ([(gl, NSA_GATE_W)], False, False)]


def _prep_even_w_in(w):
    pieces = _even_in_columns()
    scale = HEAD_DIM ** -0.5
    cols, widths = [], []
    for ranges, _, scaled in pieces:
        width = 0
        for start, wd in ranges:
            blk = w[:, start:start + wd]
            cols.append(blk * scale if scaled else blk)
            width += wd
        pad = (-width) % LANES
        if pad:
            cols.append(jnp.zeros((w.shape[0], pad), w.dtype))
        widths.append(width + pad)
    n_rope = sum(1 for _, rope, _ in pieces if rope)
    return jnp.concatenate(cols, axis=1).astype(BF16), widths, n_rope


def _prep_even_w_out(w):
    rows = [w[(g * NSA_HPG + i) * HEAD_DIM:(g * NSA_HPG + i + 1) * HEAD_DIM]
            for i in range(NSA_HPG) for g in range(NSA_GROUPS)]
    return jnp.concatenate(rows + [w[NSA_Q_W:]], axis=0).astype(BF16)


def _prep_cmp(pos, w):
    half = CMP_BLOCK // 2
    p2 = jnp.tile(pos.reshape(2, half, 1, HEAD_DIM), (1, 1, NSA_GROUPS, 1)).reshape(2, half * LANES)
    w4 = w.reshape(2, half, HEAD_DIM, HEAD_DIM).astype(BF16)
    zeros = jnp.zeros_like(w4)
    wbd = jnp.stack([jnp.concatenate([w4 if c == g else zeros for c in range(NSA_GROUPS)], axis=-1)
                     for g in range(NSA_GROUPS)], axis=2)
    return p2.astype(F32), wbd.reshape(2, half * LANES, LANES)


def _rope_tables(seq):
    inv = 1.0 / (ROPE_THETA ** (jnp.arange(0, HEAD_DIM, 2, dtype=F32) / HEAD_DIM))
    ang = jnp.arange(seq, dtype=F32)[:, None] * inv[None, :]
    cos, sin = jnp.cos(ang), jnp.sin(ang)
    reps = LANES // HEAD_DIM
    cos_t = jnp.tile(jnp.concatenate([cos, cos], axis=1), (1, reps))
    sin_t = jnp.tile(jnp.concatenate([-sin, sin], axis=1), (1, reps))
    return cos_t, sin_t


def _selection_constants(seq, sb_tk):
    n_cmp_rows = seq // CMP_STRIDE
    n_slc = seq // SEL_BLOCK
    n = np.arange(n_cmp_rows)[None, :] * CMP_STRIDE
    j = np.arange(n_slc)[:, None] * SEL_BLOCK
    ct = ((n < j + SEL_BLOCK) & (n + CMP_BLOCK > j)).astype(np.float32)
    e = np.zeros((seq, LANES), np.float32)
    e[np.arange(seq), np.arange(seq) // SEL_BLOCK] = NEG_INF
    u =(np.arange(sb_tk)[:, None] > np.arange(sb_tk)[None, :]).astype(np.float32)
    return jnp.asarray(ct, BF16), jnp.asarray(e, BF16), jnp.asarray(np.concatenate([u, u], axis=0), BF16)


def kernel(x, norm_mix, norm_ffn, norm_final, even_w_in, even_w_out, cmp_pos_k, cmp_w_k, cmp_pos_v, cmp_w_v,
           diff_lq1, diff_lk1, diff_lq2, diff_lk2, diff_subln, odd_w_in, odd_w_out, ffn_w_gate, ffn_w_up,
           ffn_w_down):
    batch, seq, _ = x.shape
    nsa_tq, nsa_tk = 256, 1024
    sb_tq, sb_tk, sb_pairs = 256, 256, 4
    diff_tq, diff_tk = 256, 1024
    assert seq % sb_tq == 0 and seq % diff_tk == 0 and seq // SEL_BLOCK <= LANES and seq // SEL_BLOCK >= SEL_TOPK
    cos_t, sin_t = _rope_tables(seq)
    ct, e_mat, u_mat = _selection_constants(seq, sb_tk)
    sb_scale = jnp.concatenate([jnp.full((D_MODEL,), HEAD_DIM ** -0.5, F32), jnp.ones((2 * D_MODEL,), F32)])

    x2 = x.reshape(batch * seq, D_MODEL)
    for layer in range(DEPTH):
        last = layer == DEPTH - 1
        if layer % 2 == 0:
            e = layer // 2
            lam_init = 0.8 - 0.6 * math.exp(-0.3 * layer)
            w_in, widths, n_rope = _prep_even_w_in(even_w_in[e])
            qn, kc, ks, kw, dq, dk, vc, vs, vw, dv, gl = _inproj(
                x2, norm_mix[layer], w_in, cos_t, sin_t, widths, n_rope, True, seq)
            pk, wk = _prep_cmp(cmp_pos_k[e], cmp_w_k[e])
            pv, wv = _prep_cmp(cmp_pos_v[e], cmp_w_v[e])
            kcmp, vcmp = _compress(kc, vc, pk, pv, wk, wv, batch, seq)
            o_nsa = _nsa(qn, kcmp, vcmp, ks, vs, kw, vw, gl, ct, e_mat, batch, seq, nsa_tq, nsa_tk)
            lam_rows = jnp.zeros((8, LANES), F32).at[0:4, 0:HEAD_DIM].set(
                jnp.stack([diff_lq1[e], diff_lk1[e], diff_lq2[e], diff_lk2[e]]))
            o_diff = _diff(dq, dk, dv, lam_rows, diff_subln[e].reshape(DIFF_VDIM, 1), lam_init, batch, seq,
                           diff_tq, diff_tk)
            wo = _prep_even_w_out(even_w_out[e])
            oa, ob, oa_blk, ob_blk = o_nsa, o_diff, 0, 0
        else:
            o = layer // 2
            w_in = (odd_w_in[o] * sb_scale[None, :]).astype(BF16)
            (qkv,) = _inproj(x2, norm_mix[layer], w_in, cos_t, sin_t, [3 * D_MODEL], 0, False, seq)
            o_sb = _sb(qkv, u_mat, batch, seq, sb_tq, sb_tk, sb_pairs)
            wo = odd_w_out[o].astype(BF16)
            oa, ob, oa_blk, ob_blk = o_sb, o_sb, 0, 1
        x2 = _mix_ffn(x2, oa, ob, oa_blk, ob_blk, wo, norm_ffn[layer],
                      _to_bf16(ffn_w_gate, layer), _to_bf16(ffn_w_up, layer),
                      _to_bf16(ffn_w_down, layer), norm_final, last)
    return x2.reshape(batch, seq, D_MODEL)
```
